```python
import math
import jax, jax.numpy as jnp
from jax import lax
import numpy as np

D_MODEL = 1024
BATCH = 32
SEQ = 256
DEPTH = 2
DEC_BATCH = 8
DEC_SEQ = 2048
PAST_LEN = 256

GRID_W = 64
N_HEADS = 8
N_KV_HEADS = 2
HEAD_DIM = 64
GROUP = N_HEADS // N_KV_HEADS
ATTN_W = N_HEADS * HEAD_DIM
KV_W = N_KV_HEADS * HEAD_DIM
WINDOW = 128
BLOCK = 128
CONV_CH = D_MODEL // 4
CONV_K = 31
FNET_GROUPS = 4
FNET_CH = D_MODEL // 4
FNET_GC = FNET_CH // FNET_GROUPS
N_BRANCH = 3
N_IN = ATTN_W + 2 * KV_W + 2 * CONV_CH + FNET_CH + N_BRANCH * D_MODEL
D_FF = int(math.ceil(8 * D_MODEL / 3 / 128)) * 128
FFN_K = 3
ROPE_THETA = 10000.0
ALPHA = (2 * DEPTH) ** 0.25
BETA = (8 * DEPTH) ** -0.25
NEG_INF = -1e30
LN_EPS = 1e-6
SPLITS = (ATTN_W, ATTN_W + KV_W, ATTN_W + 2 * KV_W, ATTN_W + 2 * KV_W + 2 * CONV_CH,
          ATTN_W + 2 * KV_W + 2 * CONV_CH + FNET_CH)

kernel_name = "hybrid_diffusion_prefix_step"


def layer_norm(x, g=None, b=None):
    xf = x.astype(jnp.float32)
    mu = jnp.mean(xf, axis=-1, keepdims=True)
    var = jnp.mean(jnp.square(xf - mu), axis=-1, keepdims=True)
    y = (xf - mu) * lax.rsqrt(var + LN_EPS)
    if g is not None:
        y = y * g.astype(jnp.float32) + b.astype(jnp.float32)
    return y.astype(x.dtype)


def dwconv(x, w, b):
    k = w.shape[0]
    pad = k // 2
    y = lax.conv_general_dilated(x, w[:, None, :].astype(x.dtype), window_strides=(1,),
                                 padding=[(pad, pad)], dimension_numbers=("NWC", "WIO", "NWC"),
                                 feature_group_count=x.shape[-1])
    return y + b


def axial_rope(length):
    n_rows = length // GRID_W
    row = jnp.repeat(jnp.arange(n_rows, dtype=jnp.float32), GRID_W)
    col = jnp.tile(jnp.arange(GRID_W, dtype=jnp.float32), n_rows)
    quarter = HEAD_DIM // 4
    inv = ROPE_THETA ** (-jnp.arange(quarter, dtype=jnp.float32) / quarter)
    ang = jnp.concatenate([row[:, None] * inv, col[:, None] * inv], axis=-1)
    return jnp.cos(ang), jnp.sin(ang)


def apply_rope(x, cos, sin):
    c = cos[None, :, None, :].astype(x.dtype)
    s = sin[None, :, None, :].astype(x.dtype)
    x1, x2 = jnp.split(x, 2, axis=-1)
    return jnp.concatenate([x1 * c - x2 * s, x2 * c + x1 * s], axis=-1)


def window_attn(q, k, v, kc, vc, sink):
    bsz, length = q.shape[:2]
    nb = length // BLOCK
    lc = kc.shape[1]
    qb = q.reshape(bsz, nb, BLOCK, N_KV_HEADS, GROUP, HEAD_DIM) * (HEAD_DIM ** -0.5)
    padw = ((0, 0), (BLOCK, BLOCK), (0, 0), (0, 0))
    kp = jnp.pad(k, padw).reshape(bsz, nb + 2, BLOCK, N_KV_HEADS, HEAD_DIM)
    vp = jnp.pad(v, padw).reshape(bsz, nb + 2, BLOCK, N_KV_HEADS, HEAD_DIM)
    kwin = jnp.concatenate([kp[:, :-2], kp[:, 1:-1], kp[:, 2:]], axis=2)
    vwin = jnp.concatenate([vp[:, :-2], vp[:, 1:-1], vp[:, 2:]], axis=2)
    s_loc = jnp.einsum("bnqkgd,bnskd->bnkgqs", qb, kwin).astype(jnp.float32)
    blk = jnp.arange(nb)[:, None, None] * BLOCK
    qpos = blk + jnp.arange(BLOCK)[None, :, None]
    kpos = blk - BLOCK + jnp.arange(3 * BLOCK)[None, None, :]
    valid = (jnp.abs(qpos - kpos) <= WINDOW) & (kpos >= 0) & (kpos < length)
    s_loc = jnp.where(valid[None, :, None, None], s_loc, NEG_INF)
    s_ctx = jnp.einsum("bnqkgd,bskd->bnkgqs", qb, kc).astype(jnp.float32)
    s_sink = jnp.broadcast_to(sink.astype(jnp.float32).reshape(1, 1, N_KV_HEADS, GROUP, 1, 1),
                              s_loc.shape[:-1] + (1,))
    p = jax.nn.softmax(jnp.concatenate([s_loc, s_ctx, s_sink], axis=-1), axis=-1)
    p_loc = p[..., :3 * BLOCK].astype(v.dtype)
    p_ctx = p[..., 3 * BLOCK:3 * BLOCK + lc].astype(vc.dtype)
    o = (jnp.einsum("bnkgqs,bnskd->bnqkgd", p_loc, vwin)
         + jnp.einsum("bnkgqs,bskd->bnqkgd", p_ctx, vc))
    return o.reshape(bsz, length, ATTN_W)


def ctx_attn(q, kc, vc, sink):
    bsz, lc = q.shape[:2]
    nb = lc // BLOCK
    qb = jnp.moveaxis(q.reshape(bsz, nb, BLOCK, N_KV_HEADS, GROUP, HEAD_DIM), 1, 0)
    sink_f = sink.astype(jnp.float32).reshape(1, N_KV_HEADS, GROUP, 1, 1)

    def one_block(qi):
        s = jnp.einsum("bqkgd,bskd->bkgqs", qi * (HEAD_DIM ** -0.5), kc).astype(jnp.float32)
        s_sink = jnp.broadcast_to(sink_f, s.shape[:-1] + (1,))
        p = jax.nn.softmax(jnp.concatenate([s, s_sink], axis=-1), axis=-1)
        return jnp.einsum("bkgqs,bskd->bqkgd", p[..., :lc].astype(vc.dtype), vc)

    o = lax.map(one_block, qb)
    return jnp.moveaxis(o, 0, 1).reshape(bsz, lc, ATTN_W)


def fourier_mix(f):
    bsz, length = f.shape[:2]
    fg = f.reshape(bsz, length, FNET_GROUPS, FNET_GC).astype(jnp.float32)
    y = jnp.real(jnp.fft.fft2(fg, axes=(1, 3), norm="ortho"))
    return y.reshape(bsz, length, FNET_CH).astype(f.dtype)


def layer(x, mod, p, latent, kc, vc):
    sh1, sc1, g1, sh2, sc2, g2 = [m[:, None, :] for m in jnp.split(mod, 6, axis=-1)]
    bsz, length = x.shape[:2]
    h = layer_norm(x) * (1 + sc1) + sh1
    z = h @ p["w_in"] + p["b_in"]
    q, k, v, conv_in, f_in, gates = jnp.split(z, SPLITS, axis=-1)
    q = q.reshape(bsz, length, N_HEADS, HEAD_DIM)
    k = k.reshape(bsz, length, N_KV_HEADS, HEAD_DIM)
    v = v.reshape(bsz, length, N_KV_HEADS, HEAD_DIM)
    if latent:
        cos, sin = axial_rope(length)
        attn = window_attn(apply_rope(q, cos, sin), apply_rope(k, cos, sin), v, kc, vc, p["sink"])
    else:
        attn = ctx_attn(q, k, v, p["sink"])
    a_attn = attn @ p["w_attn_o"]
    ua, ub = jnp.split(conv_in, 2, axis=-1)
    u = dwconv(ua * jax.nn.sigmoid(ub), p["conv_w"], p["conv_b"])
    u = jax.nn.silu(layer_norm(u, p["conv_ln_g"], p["conv_ln_b"]))
    a_conv = u @ p["w_conv_o"]
    a_f = fourier_mix(f_in) @ p["w_fnet"] + p["b_fnet"]
    ga, gc, gf = jnp.split(jax.nn.sigmoid(gates), N_BRANCH, axis=-1)
    merged = ga * a_attn + gc * a_conv + gf * a_f
    x = layer_norm(ALPHA * x + g1 * (merged @ p["w_o"]), p["ln1_g"], p["ln1_b"])
    h = layer_norm(x) * (1 + sc2) + sh2
    u = dwconv(h @ p["w_up"], p["ffn_conv_w"], p["ffn_conv_b"])
    ug, uv = jnp.split(u, 2, axis=-1)
    x = layer_norm(ALPHA * x + g2 * ((jax.nn.silu(ug) * uv) @ p["w_down"]), p["ln2_g"], p["ln2_b"])
    return x, k, v


def setup_inputs(seed: int = 0) -> dict:
    key = jax.random.key(seed)
    ks = jax.random.split(key, 32)

    def nrm(k, shape, scale):
        return jax.random.normal(k, shape, jnp.float32) * scale

    L = DEPTH
    return {
        "x_prompt": nrm(ks[0], (BATCH, SEQ, D_MODEL), 1.0),
        "x_sample": nrm(ks[1], (DEC_BATCH, DEC_SEQ, D_MODEL), 1.0),
        "cache_k": nrm(ks[2], (DEC_BATCH, DEPTH, PAST_LEN, N_KV_HEADS, HEAD_DIM), 1.0),
        "cache_v": nrm(ks[3], (DEC_BATCH, DEPTH, PAST_LEN, N_KV_HEADS, HEAD_DIM), 1.0),
        "c": nrm(ks[4], (DEC_BATCH, D_MODEL), 1.0),
        "c_ctx": nrm(ks[5], (D_MODEL,), 1.0),
        "w_mod": nrm(ks[6], (L, D_MODEL, 6 * D_MODEL), 0.5 * D_MODEL ** -0.5),
        "b_mod": nrm(ks[7], (L, 6 * D_MODEL), 0.02),
        "w_in": nrm(ks[8], (L, D_MODEL, N_IN), D_MODEL ** -0.5),
        "b_in": nrm(ks[9], (L, N_IN), 0.02),
        "sink": nrm(ks[10], (L, N_HEADS), 0.5),
        "w_attn_o": nrm(ks[11], (L, ATTN_W, D_MODEL), ATTN_W ** -0.5),
        "conv_w": nrm(ks[12], (L, CONV_K, CONV_CH), CONV_K ** -0.5),
        "conv_b": nrm(ks[13], (L, CONV_CH), 0.02),
        "conv_ln_g": 1.0 + nrm(ks[14], (L, CONV_CH), 0.02),
        "conv_ln_b": nrm(ks[15], (L, CONV_CH), 0.02),
        "w_conv_o": nrm(ks[16], (L, CONV_CH, D_MODEL), CONV_CH ** -0.5),
        "w_fnet": nrm(ks[17], (L, FNET_CH, D_MODEL), FNET_CH ** -0.5),
        "b_fnet": nrm(ks[18], (L, D_MODEL), 0.02),
        "w_o": nrm(ks[19], (L, D_MODEL, D_MODEL), BETA * D_MODEL ** -0.5),
        "ln1_g": 1.0 + nrm(ks[20], (L, D_MODEL), 0.02),
        "ln1_b": nrm(ks[21], (L, D_MODEL), 0.02),
        "w_up": nrm(ks[22], (L, D_MODEL, 2 * D_FF), D_MODEL ** -0.5),
        "ffn_conv_w": nrm(ks[23], (L, FFN_K, 2 * D_FF), FFN_K ** -0.5),
        "ffn_conv_b": nrm(ks[24], (L, 2 * D_FF), 0.02),
        "w_down": nrm(ks[25], (L, D_FF, D_MODEL), BETA * D_FF ** -0.5),
        "ln2_g": 1.0 + nrm(ks[26], (L, D_MODEL), 0.02),
        "ln2_b": nrm(ks[27], (L, D_MODEL), 0.02),
    }


def reference(x_prompt, x_sample, cache_k, cache_v, c, c_ctx, w_mod, b_mod, w_in, b_in, sink,
              w_attn_o, conv_w, conv_b, conv_ln_g, conv_ln_b, w_conv_o, w_fnet, b_fnet, w_o,
              ln1_g, ln1_b, w_up, ffn_conv_w, ffn_conv_b, w_down, ln2_g, ln2_b):
    xp = x_prompt
    xs = x_sample
    new_k = []
    new_v = []
    for l in range(DEPTH):
        p = {
            "w_in": w_in[l], "b_in": b_in[l], "sink": sink[l], "w_attn_o": w_attn_o[l],
            "conv_w": conv_w[l], "conv_b": conv_b[l], "conv_ln_g": conv_ln_g[l],
            "conv_ln_b": conv_ln_b[l], "w_conv_o": w_conv_o[l], "w_fnet": w_fnet[l],
            "b_fnet": b_fnet[l], "w_o": w_o[l], "ln1_g": ln1_g[l], "ln1_b": ln1_b[l],
            "w_up": w_up[l], "ffn_conv_w": ffn_conv_w[l], "ffn_conv_b": ffn_conv_b[l],
            "w_down": w_down[l], "ln2_g": ln2_g[l], "ln2_b": ln2_b[l],
        }
        mod_ctx = (jax.nn.silu(c_ctx) @ w_mod[l] + b_mod[l])[None, :]
        mod_lat = jax.nn.silu(c) @ w_mod[l] + b_mod[l]
        xp, kc_new, vc_new = layer(xp, mod_ctx, p, False, None, None)
        new_k.append(kc_new)
        new_v.append(vc_new)
        xs, _, _ = layer(xs, mod_lat, p, True, cache_k[:, l], cache_v[:, l])
    new_cache_k = jnp.stack(new_k, axis=1)
    new_cache_v = jnp.stack(new_v, axis=1)
    return (xp, xs, new_cache_k, new_cache_v)
```

```python
import functools
import math

import jax
import jax.numpy as jnp
import numpy as np
from jax import lax
from jax.experimental import pallas as pl
from jax.experimental.pallas import tpu as pltpu

D_MODEL = 1024
DEPTH = 2
GRID_W = 64
N_HEADS = 8
N_KV_HEADS = 2
HEAD_DIM = 64
ATTN_W = N_HEADS * HEAD_DIM
KV_W = N_KV_HEADS * HEAD_DIM
WINDOW = 128
BLOCK = 128
CONV_CH = D_MODEL // 4
CONV_K = 31
CONV_PAD = 16
FNET_GROUPS = 4
FNET_CH = D_MODEL // 4
FNET_GC = FNET_CH // FNET_GROUPS
N_GATES = 3 * D_MODEL
N_IN = ATTN_W + 2 * KV_W + 2 * CONV_CH + FNET_CH + N_GATES
D_FF = int(math.ceil(8 * D_MODEL / 3 / 128)) * 128
FFN_CHUNK = 256
ROPE_THETA = 10000.0
ALPHA = (2 * DEPTH) ** 0.25
NEG_INF = -1e30
LN_EPS = 1e-6

OFF_K = ATTN_W
OFF_V = OFF_K + KV_W
OFF_CA = OFF_V + KV_W
OFF_CB = OFF_CA + CONV_CH
OFF_F = OFF_CB + CONV_CH
OFF_G = OFF_F + FNET_CH

LANES = 128
SUBLANES = 8
VMEM_LIMIT = 56 * 1024 * 1024
MOD_ROWS = 16

F32 = jnp.float32
BF16 = jnp.bfloat16


def _resident(shape):
    nd = len(shape)
    return pl.BlockSpec(shape, lambda *_: (0,) * nd, pipeline_mode=pl.Buffered(1))


def _params(n_axes):
    return pltpu.CompilerParams(dimension_semantics=("arbitrary",) * n_axes,
                                vmem_limit_bytes=VMEM_LIMIT)


def _sigmoid(x):
    return 1.0 / (1.0 + jnp.exp(-x))


def _ln(x):
    mu = jnp.mean(x, axis=-1, keepdims=True)
    xc = x - mu
    var = jnp.mean(xc * xc, axis=-1, keepdims=True)
    return xc * lax.rsqrt(var + LN_EPS)


def _dot(a, b):
    return jnp.dot(a, b, preferred_element_type=F32)


def _mod_kernel(c_ref, w_ref, b_ref, o_ref):
    c = c_ref[...]
    a = (c * _sigmoid(c)).astype(BF16)
    o_ref[0] = _dot(a, w_ref[0].astype(BF16)) + b_ref[0]


def _modulation(cc, w_mod, b_mod):
    tn = 1536
    n_out = w_mod.shape[-1]
    return pl.pallas_call(
        _mod_kernel,
        grid=(DEPTH, n_out // tn),
        in_specs=[pl.BlockSpec((MOD_ROWS, D_MODEL), lambda l, j: (0, 0)),
                  pl.BlockSpec((1, D_MODEL, tn), lambda l, j: (l, 0, j)),
                  pl.BlockSpec((1, 1, tn), lambda l, j: (l, 0, j))],
        out_specs=pl.BlockSpec((1, MOD_ROWS, tn), lambda l, j: (l, 0, j)),
        out_shape=jax.ShapeDtypeStruct((DEPTH, MOD_ROWS, n_out), F32),
        compiler_params=_params(2),
        name="modulation",
    )(cc, w_mod, b_mod.reshape(DEPTH, 1, n_out))


def _rope_tile(t, cos, sin):
    lane = lax.broadcasted_iota(jnp.int32, t.shape, 1)
    first_half = (lane % HEAD_DIM) < (HEAD_DIM // 2)
    partner = jnp.where(first_half, pltpu.roll(t, LANES - HEAD_DIM // 2, 1),
                        pltpu.roll(t, HEAD_DIM // 2, 1))
    return t * cos + partner * sin


def _inproj_kernel(*refs, latent):
    if latent:
        (x_ref, mod_ref, w_ref, b_ref, bcs_ref, cos_ref, sin_ref,
         q_ref, k_ref, v_ref, g_ref, fcs_ref, gate_ref) = refs
    else:
        (x_ref, mod_ref, w_ref, b_ref, bcs_ref,
         q_ref, k_ref, v_ref, g_ref, fcs_ref, gate_ref) = refs
    h = _ln(x_ref[...]) * (1.0 + mod_ref[0, 1:2, :]) + mod_ref[0, 0:1, :]
    hb = h.astype(BF16)

    def seg(c0, c1):
        return _dot(hb, w_ref[:, c0:c1]) + b_ref[:, c0:c1]

    scale = HEAD_DIM ** -0.5
    for t in range(ATTN_W // LANES):
        qt = seg(t * LANES, (t + 1) * LANES)
        if latent:
            qt = _rope_tile(qt, cos_ref[...], sin_ref[...])
        q_ref[:, t * LANES:(t + 1) * LANES] = (qt * scale).astype(BF16)
    kt = seg(OFF_K, OFF_V)
    if latent:
        kt = _rope_tile(kt, cos_ref[...], sin_ref[...])
    k_ref[...] = kt
    v_ref[...] = seg(OFF_V, OFF_CA)
    g_ref[...] = seg(OFF_CA, OFF_CB) * _sigmoid(seg(OFF_CB, OFF_F))
    fcs_ref[...] = _dot(seg(OFF_F, OFF_G).astype(BF16), bcs_ref[...]).astype(BF16)
    for t in range(3):
        c0 = OFF_G + t * D_MODEL
        gate_ref[:, t * D_MODEL:(t + 1) * D_MODEL] = _sigmoid(seg(c0, c0 + D_MODEL)).astype(BF16)


def _inproj(x, mod, mod_off, mod_stride, seq, w_in, b_in, bcs, rope, tm=256):
    n_tok = x.shape[0]
    latent = rope is not None
    tok = lambda i: (i, 0)
    in_specs = [pl.BlockSpec((tm, D_MODEL), tok),
                pl.BlockSpec((1, 6, D_MODEL), lambda i: (mod_off + mod_stride * ((i * tm) // seq), 0, 0)),
                _resident((D_MODEL, N_IN)),
                _resident((1, N_IN)),
                _resident((FNET_CH, 2 * FNET_CH))]
    args = [x, mod, w_in, b_in, bcs]
    if latent:
        pos = lambda i: (i % (seq // tm), 0)
        in_specs += [pl.BlockSpec((tm, LANES), pos), pl.BlockSpec((tm, LANES), pos)]
        args += list(rope)
    widths = (ATTN_W, KV_W, KV_W, CONV_CH, 2 * FNET_CH, N_GATES)
    dtypes = (BF16, F32, F32, F32, BF16, BF16)
    return pl.pallas_call(
        functools.partial(_inproj_kernel, latent=latent),
        grid=(n_tok // tm,),
        in_specs=in_specs,
        out_specs=[pl.BlockSpec((tm, w), tok) for w in widths],
        out_shape=[jax.ShapeDtypeStruct((n_tok, w), d) for w, d in zip(widths, dtypes)],
        compiler_params=_params(1),
        name="inproj_latent" if latent else "inproj_context",
    )(*args)


def _pair_operand(x, kv):
    lane = lax.broadcasted_iota(jnp.int32, x.shape, 1)
    low = lane < HEAD_DIM
    swapped = pltpu.roll(x, HEAD_DIM, 1)
    lo_src, hi_src = (x, swapped) if kv == 0 else (swapped, x)
    lo = jnp.where(low, lo_src, 0.0)
    hi = jnp.where(low, 0.0, hi_src)
    return jnp.concatenate([lo, hi], axis=0).astype(BF16)


def _softmax_terms(s, sink):
    m = jnp.maximum(jnp.max(s, axis=-1, keepdims=True), sink)
    p = jnp.exp(s - m)
    denom = jnp.sum(p, axis=-1, keepdims=True) + jnp.exp(sink - m)
    return p, denom


def _attend(q_ref, o_ref, sink_ref, keys, vals, mask):
    n = keys.shape[0]
    kb = [_pair_operand(keys, kv) for kv in range(N_KV_HEADS)]
    vb = [_pair_operand(vals, kv) for kv in range(N_KV_HEADS)]
    lane = lax.broadcasted_iota(jnp.int32, (q_ref.shape[0], LANES), 1)
    for t in range(ATTN_W // LANES):
        kv = (2 * t) // (N_HEADS // N_KV_HEADS)
        s = lax.dot_general(q_ref[:, t * LANES:(t + 1) * LANES], kb[kv],
                            (((1,), (1,)), ((), ())), preferred_element_type=F32)
        parts = []
        inv = []
        for half in range(2):
            sh = s[:, half * n:(half + 1) * n]
            if mask is not None:
                nm = mask.shape[1]
                sh = jnp.concatenate([jnp.where(mask, sh[:, :nm], NEG_INF), sh[:, nm:]], axis=1)
            p, denom = _softmax_terms(sh, sink_ref[2 * t + half])
            parts.append(p.astype(BF16))
            inv.append(1.0 / denom)
        o = _dot(jnp.concatenate(parts, axis=1), vb[kv])
        o_ref[:, t * LANES:(t + 1) * LANES] = (
            o * jnp.where(lane < HEAD_DIM, inv[0], inv[1])).astype(o_ref.dtype)


def _ctx_attn_kernel(sink_ref, q_ref, k_ref, v_ref, o_ref):
    _attend(q_ref, o_ref, sink_ref, k_ref[...], v_ref[...], None)


def _ctx_attn(q, k, v, sink, seq):
    n_tok = q.shape[0]
    tok = lambda b: (b, 0)
    return pl.pallas_call(
        _ctx_attn_kernel,
        grid=(n_tok // seq,),
        in_specs=[pl.BlockSpec(memory_space=pltpu.SMEM),
                  pl.BlockSpec((seq, ATTN_W), tok),
                  pl.BlockSpec((seq, KV_W), tok),
                  pl.BlockSpec((seq, KV_W), tok)],
        out_specs=pl.BlockSpec((seq, ATTN_W), tok),
        out_shape=jax.ShapeDtypeStruct((n_tok, ATTN_W), BF16),
        compiler_params=_params(1),
        name="context_attention",
    )(sink, q, k, v)


def _win_attn_kernel(sink_ref, q_ref, k_ref, v_ref, ck_ref, cv_ref, o_ref, *, seq):
    qb = pl.program_id(1)
    span = 3 * BLOCK
    start = pl.multiple_of(jnp.clip(qb * BLOCK - BLOCK, 0, seq - span), BLOCK)
    keys = jnp.concatenate([k_ref[pl.ds(start, span), :], ck_ref[0, 0]], axis=0)
    vals = jnp.concatenate([v_ref[pl.ds(start, span), :], cv_ref[0, 0]], axis=0)
    qpos = qb * BLOCK + lax.broadcasted_iota(jnp.int32, (BLOCK, span), 0)
    kpos = start + lax.broadcasted_iota(jnp.int32, (BLOCK, span), 1)
    mask = jnp.abs(qpos - kpos) <= WINDOW
    _attend(q_ref, o_ref, sink_ref, keys, vals, mask)


def _win_attn(q, k, v, cache_k, cache_v, layer, sink, seq):
    n_tok = q.shape[0]
    bsz = n_tok // seq
    nb = seq // BLOCK
    past = cache_k.shape[2]
    qmap = lambda b, i: (b * nb + i, 0)
    kvmap = lambda b, i: (b, 0)
    cmap = lambda b, i: (b, layer, 0, 0)
    return pl.pallas_call(
        functools.partial(_win_attn_kernel, seq=seq),
        grid=(bsz, nb),
        in_specs=[pl.BlockSpec(memory_space=pltpu.SMEM),
                  pl.BlockSpec((BLOCK, ATTN_W), qmap),
                  pl.BlockSpec((seq, KV_W), kvmap),
                  pl.BlockSpec((seq, KV_W), kvmap),
                  pl.BlockSpec((1, 1, past, KV_W), cmap),
                  pl.BlockSpec((1, 1, past, KV_W), cmap)],
        out_specs=pl.BlockSpec((BLOCK, ATTN_W), qmap),
        out_shape=jax.ShapeDtypeStruct((n_tok, ATTN_W), BF16),
        compiler_params=_params(2),
        name="window_attention",
    )(sink, q, k, v, cache_k, cache_v)


CONV_ROWS = 32


def _mix_kernel(x_ref, mod_ref, attn_ref, g_ref, fcs_ref, cs_ref, gate_ref,
                wa_ref, wc_ref, wf_ref, wo_ref, cw_ref, cvec_ref, dvec_ref,
                o_ref, gp_ref, u_ref, *, seq, tm):
    i = pl.program_id(1)
    nblk = seq // tm
    r0 = pl.multiple_of(i * tm, tm)

    gp_ref[CONV_PAD:CONV_PAD + tm, :] = g_ref[pl.ds(r0, tm), :]
    top = g_ref[pl.ds(pl.multiple_of(jnp.maximum(r0 - CONV_PAD, 0), SUBLANES), CONV_PAD), :]
    gp_ref[0:CONV_PAD, :] = jnp.where(i > 0, top, 0.0)
    bot = g_ref[pl.ds(pl.multiple_of(jnp.minimum(r0 + tm, seq - CONV_PAD), SUBLANES), CONV_PAD), :]
    gp_ref[CONV_PAD + tm:, :] = jnp.where(i < nblk - 1, bot, 0.0)
    shift = CONV_PAD - CONV_K // 2
    for rc in range(tm // CONV_ROWS):
        base = rc * CONV_ROWS + shift
        acc = cw_ref[0:1, :] * gp_ref[base:base + CONV_ROWS, :]
        for j in range(1, CONV_K):
            acc = acc + cw_ref[j:j + 1, :] * gp_ref[base + j:base + j + CONV_ROWS, :]
        y = _ln(acc + cvec_ref[0:1, :]) * cvec_ref[1:2, :] + cvec_ref[2:3, :]
        u_ref[rc * CONV_ROWS:(rc + 1) * CONV_ROWS, :] = (y * _sigmoid(y)).astype(BF16)

    fm = _dot(cs_ref[:, 0:seq], fcs_ref[:, 0:FNET_CH]) + _dot(cs_ref[:, seq:], fcs_ref[:, FNET_CH:])

    a_attn = _dot(attn_ref[...], wa_ref[...])
    a_conv = _dot(u_ref[...], wc_ref[...])
    a_f = _dot(fm.astype(BF16), wf_ref[...]) + dvec_ref[0:1, :]
    merged = (gate_ref[:, 0:D_MODEL].astype(F32) * a_attn
              + gate_ref[:, D_MODEL:2 * D_MODEL].astype(F32) * a_conv
              + gate_ref[:, 2 * D_MODEL:].astype(F32) * a_f)
    y = _dot(merged.astype(BF16), wo_ref[...])
    r = ALPHA * x_ref[...] + mod_ref[0, 2:3, :] * y
    o_ref[...] = _ln(r) * dvec_ref[1:2, :] + dvec_ref[2:3, :]


def _mix(x, mod, mod_off, mod_stride, seq, attn, g, fcs, cs, gates, wa, wc, wf, wo, cw, cvec, dvec,
         tm=256):
    n_tok = x.shape[0]
    bsz = n_tok // seq
    nblk = seq // tm
    tok = lambda b, i: (b * nblk + i, 0)
    per_seq = lambda b, i: (b, 0)
    return pl.pallas_call(
        functools.partial(_mix_kernel, seq=seq, tm=tm),
        grid=(bsz, nblk),
        in_specs=[pl.BlockSpec((tm, D_MODEL), tok),
                  pl.BlockSpec((1, 6, D_MODEL), lambda b, i: (mod_off + mod_stride * b, 0, 0)),
                  pl.BlockSpec((tm, ATTN_W), tok),
                  pl.BlockSpec((seq, CONV_CH), per_seq),
                  pl.BlockSpec((seq, 2 * FNET_CH), per_seq),
                  pl.BlockSpec((tm, 2 * seq), lambda b, i: (i, 0)),
                  pl.BlockSpec((tm, N_GATES), tok),
                  _resident((ATTN_W, D_MODEL)),
                  _resident((CONV_CH, D_MODEL)),
                  _resident((FNET_CH, D_MODEL)),
                  _resident((D_MODEL, D_MODEL)),
                  _resident((CONV_K, CONV_CH)),
                  _resident((3, CONV_CH)),
                  _resident((3, D_MODEL))],
        out_specs=pl.BlockSpec((tm, D_MODEL), tok),
        out_shape=jax.ShapeDtypeStruct((n_tok, D_MODEL), F32),
        scratch_shapes=[pltpu.VMEM((tm + 2 * CONV_PAD, CONV_CH), F32),
                        pltpu.VMEM((tm, CONV_CH), BF16)],
        compiler_params=_params(2),
        name="mix_latent" if mod_stride else "mix_context",
    )(x, mod, attn, g, fcs, cs, gates, wa, wc, wf, wo, cw, cvec, dvec)


def _ffn_kernel(xp_ref, x_ref, xn_ref, mod_ref, wug_ref, wuv_ref, wd_ref, fw_ref, fb_ref, lvec_ref,
                o_ref, *, seq, tm):
    i = pl.program_id(0)
    r0 = i * tm
    scale = 1.0 + mod_ref[0, 4:5, :]
    shift = mod_ref[0, 3:4, :]

    def modulated(x):
        return _ln(x) * scale + shift

    x = x_ref[...]
    hp = jnp.where(r0 % seq == 0, 0.0, modulated(xp_ref[...]))
    hn = jnp.where((r0 + tm) % seq == 0, 0.0, modulated(xn_ref[...]))
    ext = jnp.concatenate([hp, modulated(x), hn], axis=0).astype(BF16)
    n_ext = tm + 2 * SUBLANES

    def conv3(u, c):
        lo, hi = c * FFN_CHUNK, (c + 1) * FFN_CHUNK
        prev = pltpu.roll(u, 1, 0)[SUBLANES:SUBLANES + tm]
        nxt = pltpu.roll(u, n_ext - 1, 0)[SUBLANES:SUBLANES + tm]
        return (fw_ref[0:1, lo:hi] * prev + fw_ref[1:2, lo:hi] * u[SUBLANES:SUBLANES + tm]
                + fw_ref[2:3, lo:hi] * nxt + fb_ref[:, lo:hi])

    n_chunk = D_FF // FFN_CHUNK
    acc = jnp.zeros((tm, D_MODEL), F32)
    for c in range(n_chunk):
        ug = conv3(_dot(ext, wug_ref[:, c * FFN_CHUNK:(c + 1) * FFN_CHUNK]), c)
        uv = conv3(_dot(ext, wuv_ref[:, c * FFN_CHUNK:(c + 1) * FFN_CHUNK]), c + n_chunk)
        act = (ug * _sigmoid(ug) * uv).astype(BF16)
        acc = acc + _dot(act, wd_ref[c * FFN_CHUNK:(c + 1) * FFN_CHUNK, :])
    r = ALPHA * x + mod_ref[0, 5:6, :] * acc
    o_ref[...] = _ln(r) * lvec_ref[0:1, :] + lvec_ref[1:2, :]


def _ffn(x, mod, mod_off, mod_stride, seq, wug, wuv, wd, fw, fb, lvec):
    n_tok = x.shape[0]
    tm = min(512, seq)
    assert seq % tm == 0
    halo = tm // SUBLANES
    last = n_tok // SUBLANES - 1
    return pl.pallas_call(
        functools.partial(_ffn_kernel, seq=seq, tm=tm),
        grid=(n_tok // tm,),
        in_specs=[pl.BlockSpec((SUBLANES, D_MODEL), lambda i: (jnp.maximum(i * halo - 1, 0), 0)),
                  pl.BlockSpec((tm, D_MODEL), lambda i: (i, 0)),
                  pl.BlockSpec((SUBLANES, D_MODEL), lambda i: (jnp.minimum((i + 1) * halo, last), 0)),
                  pl.BlockSpec((1, 6, D_MODEL), lambda i: (mod_off + mod_stride * ((i * tm) // seq), 0, 0)),
                  _resident((D_MODEL, D_FF)),
                  _resident((D_MODEL, D_FF)),
                  _resident((D_FF, D_MODEL)),
                  _resident((3, 2 * D_FF)),
                  _resident((1, 2 * D_FF)),
                  _resident((2, D_MODEL))],
        out_specs=pl.BlockSpec((tm, D_MODEL), lambda i: (i, 0)),
        out_shape=jax.ShapeDtypeStruct((n_tok, D_MODEL), F32),
        compiler_params=_params(1),
        name="ffn_latent" if mod_stride else "ffn_context",
    )(x, x, x, mod, wug, wuv, wd, fw, fb, lvec)


def _dft_cos_sin(n):
    idx = np.arange(n, dtype=np.int64)
    ang = 2.0 * np.pi * ((idx[:, None] * idx[None, :]) % n) / n
    return np.cos(ang) / np.sqrt(n), np.sin(ang) / np.sqrt(n)


def _channel_dft_table():
    c, s = _dft_cos_sin(FNET_GC)
    out = np.zeros((FNET_CH, 2 * FNET_CH), np.float32)
    for gi in range(FNET_GROUPS):
        sl = slice(gi * FNET_GC, (gi + 1) * FNET_GC)
        out[sl, sl] = c
        out[sl, FNET_CH + gi * FNET_GC:FNET_CH + (gi + 1) * FNET_GC] = s
    return jnp.asarray(out, dtype=BF16)


def _position_dft_table(n):
    c, s = _dft_cos_sin(n)
    return jnp.asarray(np.concatenate([c, -s], axis=1).astype(np.float32), dtype=BF16)


def _rope_tables(length):
    n_rows = length // GRID_W
    row = jnp.repeat(jnp.arange(n_rows, dtype=F32), GRID_W)
    col = jnp.tile(jnp.arange(GRID_W, dtype=F32), n_rows)
    quarter = HEAD_DIM // 4
    inv = ROPE_THETA ** (-jnp.arange(quarter, dtype=F32) / quarter)
    ang = jnp.concatenate([row[:, None] * inv, col[:, None] * inv], axis=-1)
    cos, sin = jnp.cos(ang), jnp.sin(ang)
    reps = LANES // HEAD_DIM
    return (jnp.tile(jnp.concatenate([cos, cos], axis=-1), (1, reps)),
            jnp.tile(jnp.concatenate([-sin, sin], axis=-1), (1, reps)))


def kernel(x_prompt, x_sample, cache_k, cache_v, c, c_ctx, w_mod, b_mod, w_in, b_in, sink,
           w_attn_o, conv_w, conv_b, conv_ln_g, conv_ln_b, w_conv_o, w_fnet, b_fnet, w_o,
           ln1_g, ln1_b, w_up, ffn_conv_w, ffn_conv_b, w_down, ln2_g, ln2_b):
    batch, seq_p, _ = x_prompt.shape
    dec_batch, seq_s, _ = x_sample.shape
    past = cache_k.shape[2]
    assert 1 + dec_batch <= MOD_ROWS

    cc = jnp.concatenate([c_ctx[None, :], c], axis=0)
    cc = jnp.pad(cc, ((0, MOD_ROWS - cc.shape[0]), (0, 0)))
    mods = _modulation(cc, w_mod, b_mod).reshape(DEPTH, MOD_ROWS, 6, D_MODEL)

    bcs = _channel_dft_table()
    cs_p = _position_dft_table(seq_p)
    cs_s = _position_dft_table(seq_s)
    rope = _rope_tables(seq_s)
    ck = cache_k.reshape(dec_batch, DEPTH, past, KV_W)
    cv = cache_v.reshape(dec_batch, DEPTH, past, KV_W)

    xp = x_prompt.reshape(batch * seq_p, D_MODEL)
    xs = x_sample.reshape(dec_batch * seq_s, D_MODEL)
    new_k, new_v = [], []
    for l in range(DEPTH):
        mod = mods[l]
        w_in_l = w_in[l].astype(BF16)
        b_in_l = b_in[l][None, :]
        wa, wc, wf, wo = (w.astype(BF16) for w in (w_attn_o[l], w_conv_o[l], w_fnet[l], w_o[l]))
        cvec = jnp.stack([conv_b[l], conv_ln_g[l], conv_ln_b[l]])
        dvec = jnp.stack([b_fnet[l], ln1_g[l], ln1_b[l]])
        wug = w_up[l, :, :D_FF].astype(BF16)
        wuv = w_up[l, :, D_FF:].astype(BF16)
        wd = w_down[l].astype(BF16)
        fb = ffn_conv_b[l][None, :]
        lvec = jnp.stack([ln2_g[l], ln2_b[l]])

        def run(x, mod_off, mod_stride, seq, rope_tabs, attend, cs):
            q, k, v, g, fcs, gates = _inproj(x, mod, mod_off, mod_stride, seq, w_in_l, b_in_l, bcs,
                                             rope_tabs)
            attn = attend(q, k, v)
            x = _mix(x, mod, mod_off, mod_stride, seq, attn, g, fcs, cs, gates, wa, wc, wf, wo,
                     conv_w[l], cvec, dvec)
            x = _ffn(x, mod, mod_off, mod_stride, seq, wug, wuv, wd, ffn_conv_w[l], fb, lvec)
            return x, k, v

        xp, kp, vp = run(xp, 0, 0, seq_p, None,
                         lambda q, k, v: _ctx_attn(q, k, v, sink[l], seq_p), cs_p)
        new_k.append(kp.reshape(batch, seq_p, N_KV_HEADS, HEAD_DIM))
        new_v.append(vp.reshape(batch, seq_p, N_KV_HEADS, HEAD_DIM))
        xs, _, _ = run(xs, 1, 1, seq_s, rope,
                       lambda q, k, v: _win_attn(q, k, v, ck, cv, l, sink[l], seq_s), cs_s)

    return (xp.reshape(batch, seq_p, D_MODEL), xs.reshape(dec_batch, seq_s, D_MODEL),
            jnp.stack(new_k, axis=1), jnp.stack(new_v, axis=1))
```

```python
import functools
import math

import jax
import jax.numpy as jnp
import numpy as np
from jax import lax
from jax.experimental import pallas as pl
from jax.experimental.pallas import tpu as pltpu

D_MODEL = 1024
DEPTH = 2
GRID_W = 64
N_HEADS = 8
N_KV_HEADS = 2
HEAD_DIM = 64
ATTN_W = N_HEADS * HEAD_DIM
KV_W = N_KV_HEADS * HEAD_DIM
WINDOW = 128
BLOCK = 128
CONV_CH = D_MODEL // 4
CONV_K = 31
CONV_PAD = 16
FNET_GROUPS = 4
FNET_CH = D_MODEL // 4
FNET_GC = FNET_CH // FNET_GROUPS
N_GATES = 3 * D_MODEL
N_IN = ATTN_W + 2 * KV_W + 2 * CONV_CH + FNET_CH + N_GATES
D_FF = int(math.ceil(8 * D_MODEL / 3 / 128)) * 128
FFN_CHUNK = 256
FFN_ROWS = 64
ROPE_THETA = 10000.0
ALPHA = (2 * DEPTH) ** 0.25
NEG_INF = -1e30
LN_EPS = 1e-6

OFF_K = ATTN_W
OFF_V = OFF_K + KV_W
OFF_CA = OFF_V + KV_W
OFF_CB = OFF_CA + CONV_CH
OFF_F = OFF_CB + CONV_CH
OFF_G = OFF_F + FNET_CH

LANES = 128
SUBLANES = 8
VMEM_LIMIT = 56 * 1024 * 1024
MOD_ROWS = 16

F32 = jnp.float32
BF16 = jnp.bfloat16


def _resident(shape):
    nd = len(shape)
    return pl.BlockSpec(shape, lambda *_: (0,) * nd, pipeline_mode=pl.Buffered(1))


def _params(n_axes):
    return pltpu.CompilerParams(dimension_semantics=("arbitrary",) * n_axes,
                                vmem_limit_bytes=VMEM_LIMIT)


def _sigmoid(x):
    return 1.0 / (1.0 + jnp.exp(-x))


def _ln(x):
    mu = jnp.mean(x, axis=-1, keepdims=True)
    xc = x - mu
    var = jnp.mean(xc * xc, axis=-1, keepdims=True)
    return xc * lax.rsqrt(var + LN_EPS)


def _dot(a, b):
    return jnp.dot(a, b, preferred_element_type=F32)


def _mod_kernel(c_ref, w_ref, b_ref, o_ref):
    c = c_ref[...]
    a = (c * _sigmoid(c)).astype(BF16)
    o_ref[0] = _dot(a, w_ref[0].astype(BF16)) + b_ref[0]


def _modulation(cc, w_mod, b_mod):
    tn = 1536
    n_out = w_mod.shape[-1]
    return pl.pallas_call(
        _mod_kernel,
        grid=(DEPTH, n_out // tn),
        in_specs=[pl.BlockSpec((MOD_ROWS, D_MODEL), lambda l, j: (0, 0)),
                  pl.BlockSpec((1, D_MODEL, tn), lambda l, j: (l, 0, j)),
                  pl.BlockSpec((1, 1, tn), lambda l, j: (l, 0, j))],
        out_specs=pl.BlockSpec((1, MOD_ROWS, tn), lambda l, j: (l, 0, j)),
        out_shape=jax.ShapeDtypeStruct((DEPTH, MOD_ROWS, n_out), F32),
        compiler_params=_params(2),
        name="modulation",
    )(cc, w_mod, b_mod.reshape(DEPTH, 1, n_out))


def _rope_tile(t, cos, sin):
    lane = lax.broadcasted_iota(jnp.int32, t.shape, 1)
    first_half = (lane % HEAD_DIM) < (HEAD_DIM // 2)
    partner = jnp.where(first_half, pltpu.roll(t, LANES - HEAD_DIM // 2, 1),
                        pltpu.roll(t, HEAD_DIM // 2, 1))
    return t * cos + partner * sin


def _inproj_kernel(*refs, latent):
    if latent:
        (x_ref, mod_ref, w_ref, b_ref, bcs_ref, cos_ref, sin_ref,
         q_ref, k_ref, v_ref, g_ref, fcs_ref, gate_ref) = refs
    else:
        (x_ref, mod_ref, w_ref, b_ref, bcs_ref,
         q_ref, k_ref, v_ref, g_ref, fcs_ref, gate_ref) = refs
    h = _ln(x_ref[...]) * (1.0 + mod_ref[0, 1:2, :]) + mod_ref[0, 0:1, :]
    hb = h.astype(BF16)

    def seg(c0, c1):
        return _dot(hb, w_ref[:, c0:c1]) + b_ref[:, c0:c1]

    scale = HEAD_DIM ** -0.5
    for t in range(ATTN_W // LANES):
        qt = seg(t * LANES, (t + 1) * LANES)
        if latent:
            qt = _rope_tile(qt, cos_ref[...], sin_ref[...])
        q_ref[:, t * LANES:(t + 1) * LANES] = (qt * scale).astype(BF16)
    kt = seg(OFF_K, OFF_V)
    if latent:
        kt = _rope_tile(kt, cos_ref[...], sin_ref[...])
    k_ref[...] = kt
    v_ref[...] = seg(OFF_V, OFF_CA)
    g_ref[...] = seg(OFF_CA, OFF_CB) * _sigmoid(seg(OFF_CB, OFF_F))
    fcs_ref[...] = _dot(seg(OFF_F, OFF_G).astype(BF16), bcs_ref[...]).astype(BF16)
    for t in range(3):
        c0 = OFF_G + t * D_MODEL
        gate_ref[:, t * D_MODEL:(t + 1) * D_MODEL] = _sigmoid(seg(c0, c0 + D_MODEL)).astype(BF16)


def _inproj(x, mod, mod_off, mod_stride, seq, w_in, b_in, bcs, rope, tm=256):
    n_tok = x.shape[0]
    latent = rope is not None
    tok = lambda i: (i, 0)
    in_specs = [pl.BlockSpec((tm, D_MODEL), tok),
                pl.BlockSpec((1, 6, D_MODEL), lambda i: (mod_off + mod_stride * ((i * tm) // seq), 0, 0)),
                _resident((D_MODEL, N_IN)),
                _resident((1, N_IN)),
                _resident((FNET_CH, 2 * FNET_CH))]
    args = [x, mod, w_in, b_in, bcs]
    if latent:
        pos = lambda i: (i % (seq // tm), 0)
        in_specs += [pl.BlockSpec((tm, LANES), pos), pl.BlockSpec((tm, LANES), pos)]
        args += list(rope)
    widths = (ATTN_W, KV_W, KV_W, CONV_CH, 2 * FNET_CH, N_GATES)
    dtypes = (BF16, F32, F32, F32, BF16, BF16)
    return pl.pallas_call(
        functools.partial(_inproj_kernel, latent=latent),
        grid=(n_tok // tm,),
        in_specs=in_specs,
        out_specs=[pl.BlockSpec((tm, w), tok) for w in widths],
        out_shape=[jax.ShapeDtypeStruct((n_tok, w), d) for w, d in zip(widths, dtypes)],
        compiler_params=_params(1),
        name="inproj_latent" if latent else "inproj_context",
    )(*args)


def _pair_halves(x, kv):
    lane = lax.broadcasted_iota(jnp.int32, x.shape, 1)
    low = lane < HEAD_DIM
    swapped = pltpu.roll(x, HEAD_DIM, 1)
    lo_src, hi_src = (x, swapped) if kv == 0 else (swapped, x)
    return jnp.where(low, lo_src, 0.0).astype(BF16), jnp.where(low, 0.0, hi_src).astype(BF16)


def _softmax_terms(s, sink):
    m = jnp.maximum(jnp.max(s, axis=-1, keepdims=True), sink)
    p = jnp.exp(s - m)
    denom = jnp.sum(p, axis=-1, keepdims=True) + jnp.exp(sink - m)
    return p, denom


def _attend_group(q_ref, o_ref, sink_ref, kv, kb, vb, mask):
    tq = q_ref.shape[0]
    n = kb.shape[0] // 2
    t0 = kv * (N_HEADS // N_KV_HEADS) // 2
    q2 = jnp.concatenate([q_ref[:, t0 * LANES:(t0 + 1) * LANES],
                          q_ref[:, (t0 + 1) * LANES:(t0 + 2) * LANES]], axis=0)
    s = lax.dot_general(q2, kb, (((1,), (1,)), ((), ())), preferred_element_type=F32)
    first_tile = lax.broadcasted_iota(jnp.int32, (2 * tq, 1), 0) < tq
    parts, inv = [], []
    for half in range(2):
        sh = s[:, half * n:(half + 1) * n]
        if mask is not None:
            nm = mask.shape[1]
            sh = jnp.concatenate([jnp.where(mask, sh[:, :nm], NEG_INF), sh[:, nm:]], axis=1)
        head = 2 * t0 + half
        sink = jnp.where(first_tile, sink_ref[head], sink_ref[head + 2])
        p, denom = _softmax_terms(sh, sink)
        parts.append(p.astype(BF16))
        inv.append(1.0 / denom)
    o = _dot(jnp.concatenate(parts, axis=1), vb)
    lane = lax.broadcasted_iota(jnp.int32, o.shape, 1)
    o = (o * jnp.where(lane < HEAD_DIM, inv[0], inv[1])).astype(o_ref.dtype)
    o_ref[:, t0 * LANES:(t0 + 1) * LANES] = o[:tq]
    o_ref[:, (t0 + 1) * LANES:(t0 + 2) * LANES] = o[tq:]


def _ctx_attn_kernel(sink_ref, q_ref, k_ref, v_ref, o_ref):
    for kv in range(N_KV_HEADS):
        kb = jnp.concatenate(_pair_halves(k_ref[...], kv), axis=0)
        vb = jnp.concatenate(_pair_halves(v_ref[...], kv), axis=0)
        _attend_group(q_ref, o_ref, sink_ref, kv, kb, vb, None)


def _ctx_attn(q, k, v, sink, seq):
    n_tok = q.shape[0]
    tok = lambda b: (b, 0)
    return pl.pallas_call(
        _ctx_attn_kernel,
        grid=(n_tok // seq,),
        in_specs=[pl.BlockSpec(memory_space=pltpu.SMEM),
                  pl.BlockSpec((seq, ATTN_W), tok),
                  pl.BlockSpec((seq, KV_W), tok),
                  pl.BlockSpec((seq, KV_W), tok)],
        out_specs=pl.BlockSpec((seq, ATTN_W), tok),
        out_shape=jax.ShapeDtypeStruct((n_tok, ATTN_W), BF16),
        compiler_params=_params(1),
        name="context_attention",
    )(sink, q, k, v)


def _win_attn_kernel(sink_ref, q_ref, k_ref, v_ref, ck_ref, cv_ref, o_ref,
                     kl_ref, vl_ref, kc_ref, vc_ref, *, seq):
    qb = pl.program_id(1)

    @pl.when(qb == 0)
    def _():
        for kv in range(N_KV_HEADS):
            for src, dst in ((k_ref[...], kl_ref), (v_ref[...], vl_ref),
                             (ck_ref[0, 0], kc_ref), (cv_ref[0, 0], vc_ref)):
                lo, hi = _pair_halves(src, kv)
                dst[kv, 0] = lo
                dst[kv, 1] = hi

    span = 3 * BLOCK
    start = pl.multiple_of(jnp.clip(qb * BLOCK - BLOCK, 0, seq - span), BLOCK)
    qpos = qb * BLOCK + lax.broadcasted_iota(jnp.int32, (BLOCK, span), 0)
    kpos = start + lax.broadcasted_iota(jnp.int32, (BLOCK, span), 1)
    mask = jnp.abs(qpos - kpos) <= WINDOW
    mask = jnp.concatenate([mask, mask], axis=0)
    for kv in range(N_KV_HEADS):
        kb = jnp.concatenate([kl_ref[kv, 0, pl.ds(start, span), :], kc_ref[kv, 0],
                              kl_ref[kv, 1, pl.ds(start, span), :], kc_ref[kv, 1]], axis=0)
        vb = jnp.concatenate([vl_ref[kv, 0, pl.ds(start, span), :], vc_ref[kv, 0],
                              vl_ref[kv, 1, pl.ds(start, span), :], vc_ref[kv, 1]], axis=0)
        _attend_group(q_ref, o_ref, sink_ref, kv, kb, vb, mask)


def _win_attn(q, k, v, cache_k, cache_v, layer, sink, seq):
    n_tok = q.shape[0]
    bsz = n_tok // seq
    nb = seq // BLOCK
    past = cache_k.shape[2]
    qmap = lambda b, i: (b * nb + i, 0)
    kvmap = lambda b, i: (b, 0)
    cmap = lambda b, i: (b, layer, 0, 0)
    return pl.pallas_call(
        functools.partial(_win_attn_kernel, seq=seq),
        grid=(bsz, nb),
        in_specs=[pl.BlockSpec(memory_space=pltpu.SMEM),
                  pl.BlockSpec((BLOCK, ATTN_W), qmap),
                  pl.BlockSpec((seq, KV_W), kvmap),
                  pl.BlockSpec((seq, KV_W), kvmap),
                  pl.BlockSpec((1, 1, past, KV_W), cmap),
                  pl.BlockSpec((1, 1, past, KV_W), cmap)],
        out_specs=pl.BlockSpec((BLOCK, ATTN_W), qmap),
        out_shape=jax.ShapeDtypeStruct((n_tok, ATTN_W), BF16),
        scratch_shapes=[pltpu.VMEM((N_KV_HEADS, 2, seq, KV_W), BF16)] * 2
        + [pltpu.VMEM((N_KV_HEADS, 2, past, KV_W), BF16)] * 2,
        compiler_params=_params(2),
        name="window_attention",
    )(sink, q, k, v, cache_k, cache_v)


CONV_ROWS = 32


def _mix_kernel(x_ref, mod_ref, attn_ref, g_ref, fcs_ref, cs_ref, gate_ref,
                wa_ref, wc_ref, wf_ref, wo_ref, cw_ref, cvec_ref, dvec_ref,
                o_ref, gp_ref, gs_ref, u_ref, *, seq, tm):
    i = pl.program_id(1)
    nblk = seq // tm
    r0 = pl.multiple_of(i * tm, tm)

    gp_ref[CONV_PAD:CONV_PAD + tm, :] = g_ref[pl.ds(r0, tm), :]
    top = g_ref[pl.ds(pl.multiple_of(jnp.maximum(r0 - CONV_PAD, 0), SUBLANES), CONV_PAD), :]
    gp_ref[0:CONV_PAD, :] = jnp.where(i > 0, top, 0.0)
    bot = g_ref[pl.ds(pl.multiple_of(jnp.minimum(r0 + tm, seq - CONV_PAD), SUBLANES), CONV_PAD), :]
    gp_ref[CONV_PAD + tm:, :] = jnp.where(i < nblk - 1, bot, 0.0)
    n_pad = tm + 2 * CONV_PAD
    staged = gp_ref[...]
    for r in range(1, SUBLANES):
        gs_ref[r - 1] = pltpu.roll(staged, n_pad - r, 0)
    first = CONV_PAD - CONV_K // 2
    for rc in range(tm // CONV_ROWS):
        acc = None
        for j in range(CONV_K):
            a, r = divmod(first + j, SUBLANES)
            src = gp_ref if r == 0 else gs_ref.at[r - 1]
            lo = rc * CONV_ROWS + a * SUBLANES
            term = cw_ref[j:j + 1, :] * src[lo:lo + CONV_ROWS, :]
            acc = term if acc is None else acc + term
        y = _ln(acc + cvec_ref[0:1, :]) * cvec_ref[1:2, :] + cvec_ref[2:3, :]
        u_ref[rc * CONV_ROWS:(rc + 1) * CONV_ROWS, :] = (y * _sigmoid(y)).astype(BF16)

    fm = _dot(cs_ref[:, 0:seq], fcs_ref[:, 0:FNET_CH]) + _dot(cs_ref[:, seq:], fcs_ref[:, FNET_CH:])

    a_attn = _dot(attn_ref[...], wa_ref[...])
    a_conv = _dot(u_ref[...], wc_ref[...])
    a_f = _dot(fm.astype(BF16), wf_ref[...]) + dvec_ref[0:1, :]
    merged = (gate_ref[:, 0:D_MODEL].astype(F32) * a_attn
              + gate_ref[:, D_MODEL:2 * D_MODEL].astype(F32) * a_conv
              + gate_ref[:, 2 * D_MODEL:].astype(F32) * a_f)
    y = _dot(merged.astype(BF16), wo_ref[...])
    r = ALPHA * x_ref[...] + mod_ref[0, 2:3, :] * y
    o_ref[...] = _ln(r) * dvec_ref[1:2, :] + dvec_ref[2:3, :]


def _mix(x, mod, mod_off, mod_stride, seq, attn, g, fcs, cs, gates, wa, wc, wf, wo, cw, cvec, dvec,
         tm=256):
    n_tok = x.shape[0]
    bsz = n_tok // seq
    nblk = seq // tm
    tok = lambda b, i: (b * nblk + i, 0)
    per_seq = lambda b, i: (b, 0)
    return pl.pallas_call(
        functools.partial(_mix_kernel, seq=seq, tm=tm),
        grid=(bsz, nblk),
        in_specs=[pl.BlockSpec((tm, D_MODEL), tok),
                  pl.BlockSpec((1, 6, D_MODEL), lambda b, i: (mod_off + mod_stride * b, 0, 0)),
                  pl.BlockSpec((tm, ATTN_W), tok),
                  pl.BlockSpec((seq, CONV_CH), per_seq),
                  pl.BlockSpec((seq, 2 * FNET_CH), per_seq),
                  pl.BlockSpec((tm, 2 * seq), lambda b, i: (i, 0)),
                  pl.BlockSpec((tm, N_GATES), tok),
                  _resident((ATTN_W, D_MODEL)),
                  _resident((CONV_CH, D_MODEL)),
                  _resident((FNET_CH, D_MODEL)),
                  _resident((D_MODEL, D_MODEL)),
                  _resident((CONV_K, CONV_CH)),
                  _resident((3, CONV_CH)),
                  _resident((3, D_MODEL))],
        out_specs=pl.BlockSpec((tm, D_MODEL), tok),
        out_shape=jax.ShapeDtypeStruct((n_tok, D_MODEL), F32),
        scratch_shapes=[pltpu.VMEM((tm + 2 * CONV_PAD, CONV_CH), F32),
                        pltpu.VMEM((SUBLANES - 1, tm + 2 * CONV_PAD, CONV_CH), F32),
                        pltpu.VMEM((tm, CONV_CH), BF16)],
        compiler_params=_params(2),
        name="mix_latent" if mod_stride else "mix_context",
    )(x, mod, attn, g, fcs, cs, gates, wa, wc, wf, wo, cw, cvec, dvec)


def _ffn_kernel(xp_ref, x_ref, xn_ref, mod_ref, wug_ref, wuv_ref, wd_ref, fw_ref, fb_ref, lvec_ref,
                o_ref, ext_ref, ug0_ref, uv0_ref, ug1_ref, uv1_ref, act_ref, *, seq, tm):
    i = pl.program_id(0)
    r0 = i * tm
    scale = 1.0 + mod_ref[0, 4:5, :]
    shift = mod_ref[0, 3:4, :]

    def modulated(x):
        return _ln(x) * scale + shift

    hp = jnp.where(r0 % seq == 0, 0.0, modulated(xp_ref[...]))
    hn = jnp.where((r0 + tm) % seq == 0, 0.0, modulated(xn_ref[...]))
    ext_ref[...] = jnp.concatenate([hp, modulated(x_ref[...]), hn], axis=0).astype(BF16)

    n_chunk = D_FF // FFN_CHUNK
    u_refs = ((ug0_ref, uv0_ref), (ug1_ref, uv1_ref))
    piece = FFN_ROWS + 2 * SUBLANES

    def up(c):
        g_ref, v_ref = u_refs[c % 2]
        lo = c * FFN_CHUNK
        g_ref[...] = _dot(ext_ref[...], wug_ref[:, lo:lo + FFN_CHUNK])
        v_ref[...] = _dot(ext_ref[...], wuv_ref[:, lo:lo + FFN_CHUNK])

    def conv3(u_ref, p, col):
        lo = col * FFN_CHUNK
        blk = u_ref[p * FFN_ROWS:p * FFN_ROWS + piece, :]
        prev = pltpu.roll(blk, 1, 0)[SUBLANES:SUBLANES + FFN_ROWS]
        nxt = pltpu.roll(blk, piece - 1, 0)[SUBLANES:SUBLANES + FFN_ROWS]
        return (fw_ref[0:1, lo:lo + FFN_CHUNK] * prev
                + fw_ref[1:2, lo:lo + FFN_CHUNK] * blk[SUBLANES:SUBLANES + FFN_ROWS]
                + fw_ref[2:3, lo:lo + FFN_CHUNK] * nxt + fb_ref[:, lo:lo + FFN_CHUNK])

    def activate(c):
        g_ref, v_ref = u_refs[c % 2]
        for p in range(tm // FFN_ROWS):
            yg = conv3(g_ref, p, c)
            yv = conv3(v_ref, p, c + n_chunk)
            act_ref[p * FFN_ROWS:(p + 1) * FFN_ROWS, c * FFN_CHUNK:(c + 1) * FFN_CHUNK] = (
                yg * _sigmoid(yg) * yv).astype(BF16)

    up(0)
    for c in range(n_chunk):
        if c + 1 < n_chunk:
            up(c + 1)
        activate(c)
    y = _dot(act_ref[...], wd_ref[...])
    r = ALPHA * x_ref[...] + mod_ref[0, 5:6, :] * y
    o_ref[...] = _ln(r) * lvec_ref[0:1, :] + lvec_ref[1:2, :]


def _ffn(x, mod, mod_off, mod_stride, seq, wug, wuv, wd, fw, fb, lvec):
    n_tok = x.shape[0]
    tm = min(512, seq)
    assert seq % tm == 0
    halo = tm // SUBLANES
    last = n_tok // SUBLANES - 1
    n_ext = tm + 2 * SUBLANES
    return pl.pallas_call(
        functools.partial(_ffn_kernel, seq=seq, tm=tm),
        grid=(n_tok // tm,),
        in_specs=[pl.BlockSpec((SUBLANES, D_MODEL), lambda i: (jnp.maximum(i * halo - 1, 0), 0)),
                  pl.BlockSpec((tm, D_MODEL), lambda i: (i, 0)),
                  pl.BlockSpec((SUBLANES, D_MODEL), lambda i: (jnp.minimum((i + 1) * halo, last), 0)),
                  pl.BlockSpec((1, 6, D_MODEL), lambda i: (mod_off + mod_stride * ((i * tm) // seq), 0, 0)),
                  _resident((D_MODEL, D_FF)),
                  _resident((D_MODEL, D_FF)),
                  _resident((D_FF, D_MODEL)),
                  _resident((3, 2 * D_FF)),
                  _resident((1, 2 * D_FF)),
                  _resident((2, D_MODEL))],
        out_specs=pl.BlockSpec((tm, D_MODEL), lambda i: (i, 0)),
        out_shape=jax.ShapeDtypeStruct((n_tok, D_MODEL), F32),
        scratch_shapes=[pltpu.VMEM((n_ext, D_MODEL), BF16)]
        + [pltpu.VMEM((n_ext, FFN_CHUNK), F32)] * 4
        + [pltpu.VMEM((tm, D_FF), BF16)],
        compiler_params=_params(1),
        name="ffn_latent" if mod_stride else "ffn_context",
    )(x, x, x, mod, wug, wuv, wd, fw, fb, lvec)


def _dft_cos_sin(n):
    idx = np.arange(n, dtype=np.int64)
    ang = 2.0 * np.pi * ((idx[:, None] * idx[None, :]) % n) / n
    return np.cos(ang) / np.sqrt(n), np.sin(ang) / np.sqrt(n)


def _channel_dft_table():
    c, s = _dft_cos_sin(FNET_GC)
    out = np.zeros((FNET_CH, 2 * FNET_CH), np.float32)
    for gi in range(FNET_GROUPS):
        sl = slice(gi * FNET_GC, (gi + 1) * FNET_GC)
        out[sl, sl] = c
        out[sl, FNET_CH + gi * FNET_GC:FNET_CH + (gi + 1) * FNET_GC] = s
    return jnp.asarray(out, dtype=BF16)


def _position_dft_table(n):
    c, s = _dft_cos_sin(n)
    return jnp.asarray(np.concatenate([c, -s], axis=1).astype(np.float32), dtype=BF16)


def _rope_tables(length):
    n_rows = length // GRID_W
    row = jnp.repeat(jnp.arange(n_rows, dtype=F32), GRID_W)
    col = jnp.tile(jnp.arange(GRID_W, dtype=F32), n_rows)
    quarter = HEAD_DIM // 4
    inv = ROPE_THETA ** (-jnp.arange(quarter, dtype=F32) / quarter)
    ang = jnp.concatenate([row[:, None] * inv, col[:, None] * inv], axis=-1)
    cos, sin = jnp.cos(ang), jnp.sin(ang)
    reps = LANES // HEAD_DIM
    return (jnp.tile(jnp.concatenate([cos, cos], axis=-1), (1, reps)),
            jnp.tile(jnp.concatenate([-sin, sin], axis=-1), (1, reps)))


def kernel(x_prompt, x_sample, cache_k, cache_v, c, c_ctx, w_mod, b_mod, w_in, b_in, sink,
           w_attn_o, conv_w, conv_b, conv_ln_g, conv_ln_b, w_conv_o, w_fnet, b_fnet, w_o,
           ln1_g, ln1_b, w_up, ffn_conv_w, ffn_conv_b, w_down, ln2_g, ln2_b):
    batch, seq_p, _ = x_prompt.shape
    dec_batch, seq_s, _ = x_sample.shape
    past = cache_k.shape[2]
    assert 1 + dec_batch <= MOD_ROWS

    cc = jnp.concatenate([c_ctx[None, :], c], axis=0)
    cc = jnp.pad(cc, ((0, MOD_ROWS - cc.shape[0]), (0, 0)))
    mods = _modulation(cc, w_mod, b_mod).reshape(DEPTH, MOD_ROWS, 6, D_MODEL)

    bcs = _channel_dft_table()
    cs_p = _position_dft_table(seq_p)
    cs_s = _position_dft_table(seq_s)
    rope = _rope_tables(seq_s)
    ck = cache_k.reshape(dec_batch, DEPTH, past, KV_W)
    cv = cache_v.reshape(dec_batch, DEPTH, past, KV_W)

    xp = x_prompt.reshape(batch * seq_p, D_MODEL)
    xs = x_sample.reshape(dec_batch * seq_s, D_MODEL)
    new_k, new_v = [], []
    for l in range(DEPTH):
        mod = mods[l]
        w_in_l = w_in[l].astype(BF16)
        b_in_l = b_in[l][None, :]
        wa, wc, wf, wo = (w.astype(BF16) for w in (w_attn_o[l], w_conv_o[l], w_fnet[l], w_o[l]))
        cvec = jnp.stack([conv_b[l], conv_ln_g[l], conv_ln_b[l]])
        dvec = jnp.stack([b_fnet[l], ln1_g[l], ln1_b[l]])
        wug = w_up[l, :, :D_FF].astype(BF16)
        wuv = w_up[l, :, D_FF:].astype(BF16)
        wd = w_down[l].astype(BF16)
        fb = ffn_conv_b[l][None, :]
        lvec = jnp.stack([ln2_g[l], ln2_b[l]])

        def run(x, mod_off, mod_stride, seq, rope_tabs, attend, cs):
            q, k, v, g, fcs, gates = _inproj(x, mod, mod_off, mod_stride, seq, w_in_l, b_in_l, bcs,
                                             rope_tabs)
            attn = attend(q, k, v)
            x = _mix(x, mod, mod_off, mod_stride, seq, attn, g, fcs, cs, gates, wa, wc, wf, wo,
                     conv_w[l], cvec, dvec)
            x = _ffn(x, mod, mod_off, mod_stride, seq, wug, wuv, wd, ffn_conv_w[l], fb, lvec)
            return x, k, v

        xp, kp, vp = run(xp, 0, 0, seq_p, None,
                         lambda q, k, v: _ctx_attn(q, k, v, sink[l], seq_p), cs_p)
        new_k.append(kp.reshape(batch, seq_p, N_KV_HEADS, HEAD_DIM))
        new_v.append(vp.reshape(batch, seq_p, N_KV_HEADS, HEAD_DIM))
        xs, _, _ = run(xs, 1, 1, seq_s, rope,
                       lambda q, k, v: _win_attn(q, k, v, ck, cv, l, sink[l], seq_s), cs_s)

    return (xp.reshape(batch, seq_p, D_MODEL), xs.reshape(dec_batch, seq_s, D_MODEL),
            jnp.stack(new_k, axis=1), jnp.stack(new_v, axis=1))
```

```python
import functools
import math

import jax
import jax.numpy as jnp
import numpy as np
from jax import lax
from jax.experimental import pallas as pl
from jax.experimental.pallas import tpu as pltpu

D_MODEL = 1024
DEPTH = 2
GRID_W = 64
N_HEADS = 8
N_KV_HEADS = 2
HEAD_DIM = 64
ATTN_W = N_HEADS * HEAD_DIM
KV_W = N_KV_HEADS * HEAD_DIM
WINDOW = 128
BLOCK = 128
CONV_CH = D_MODEL // 4
CONV_K = 31
CONV_PAD = 16
FNET_GROUPS = 4
FNET_CH = D_MODEL // 4
FNET_GC = FNET_CH // FNET_GROUPS
N_GATES = 3 * D_MODEL
N_IN = ATTN_W + 2 * KV_W + 2 * CONV_CH + FNET_CH + N_GATES
D_FF = int(math.ceil(8 * D_MODEL / 3 / 128)) * 128
FFN_CHUNK = 256
FFN_ROWS = 64
ROPE_THETA = 10000.0
ALPHA = (2 * DEPTH) ** 0.25
NEG_INF = -1e30
LN_EPS = 1e-6

OFF_K = ATTN_W
OFF_V = OFF_K + KV_W
OFF_CA = OFF_V + KV_W
OFF_CB = OFF_CA + CONV_CH
OFF_F = OFF_CB + CONV_CH
OFF_G = OFF_F + FNET_CH

LANES = 128
SUBLANES = 8
VMEM_LIMIT = 56 * 1024 * 1024
MOD_ROWS = 16
ROW_PIECE = 32

F32 = jnp.float32
BF16 = jnp.bfloat16


def _resident(shape):
    nd = len(shape)
    return pl.BlockSpec(shape, lambda *_: (0,) * nd, pipeline_mode=pl.Buffered(1))


def _params(n_axes):
    return pltpu.CompilerParams(dimension_semantics=("arbitrary",) * n_axes,
                                vmem_limit_bytes=VMEM_LIMIT)


def _sigmoid(x):
    return 1.0 / (1.0 + jnp.exp(-x))


def _ln(x):
    mu = jnp.mean(x, axis=-1, keepdims=True)
    xc = x - mu
    var = jnp.mean(xc * xc, axis=-1, keepdims=True)
    return xc * lax.rsqrt(var + LN_EPS)


def _dot(a, b):
    return jnp.dot(a, b, preferred_element_type=F32)


def _mod_kernel(c_ref, w_ref, b_ref, o_ref):
    c = c_ref[...]
    a = (c * _sigmoid(c)).astype(BF16)
    o_ref[0] = _dot(a, w_ref[0].astype(BF16)) + b_ref[0]


def _modulation(cc, w_mod, b_mod):
    tn = 1536
    n_out = w_mod.shape[-1]
    return pl.pallas_call(
        _mod_kernel,
        grid=(DEPTH, n_out // tn),
        in_specs=[pl.BlockSpec((MOD_ROWS, D_MODEL), lambda l, j: (0, 0)),
                  pl.BlockSpec((1, D_MODEL, tn), lambda l, j: (l, 0, j)),
                  pl.BlockSpec((1, 1, tn), lambda l, j: (l, 0, j))],
        out_specs=pl.BlockSpec((1, MOD_ROWS, tn), lambda l, j: (l, 0, j)),
        out_shape=jax.ShapeDtypeStruct((DEPTH, MOD_ROWS, n_out), F32),
        compiler_params=_params(2),
        name="modulation",
    )(cc, w_mod, b_mod.reshape(DEPTH, 1, n_out))


def _rope_tile(t, cos, sin):
    lane = lax.broadcasted_iota(jnp.int32, t.shape, 1)
    first_half = (lane % HEAD_DIM) < (HEAD_DIM // 2)
    partner = jnp.where(first_half, pltpu.roll(t, LANES - HEAD_DIM // 2, 1),
                        pltpu.roll(t, HEAD_DIM // 2, 1))
    return t * cos + partner * sin


def _inproj_kernel(*refs, latent):
    if latent:
        (x_ref, mod_ref, w_ref, b_ref, bcs_ref, cos_ref, sin_ref,
         q_ref, k_ref, v_ref, g_ref, fcs_ref, gate_ref, hb_ref, z_ref) = refs
    else:
        (x_ref, mod_ref, w_ref, b_ref, bcs_ref,
         q_ref, k_ref, v_ref, g_ref, fcs_ref, gate_ref, hb_ref, z_ref) = refs
    tm = x_ref.shape[0]
    scale = 1.0 + mod_ref[0, 1:2, :]
    shift = mod_ref[0, 0:1, :]
    for rp in range(tm // ROW_PIECE):
        rows = slice(rp * ROW_PIECE, (rp + 1) * ROW_PIECE)
        hb_ref[rows, :] = (_ln(x_ref[rows, :]) * scale + shift).astype(BF16)

    z_ref[...] = _dot(hb_ref[...], w_ref[:, 0:OFF_G]) + b_ref[:, 0:OFF_G]
    for t in range(3):
        c0 = OFF_G + t * D_MODEL
        gate_ref[:, t * D_MODEL:(t + 1) * D_MODEL] = _sigmoid(
            _dot(hb_ref[...], w_ref[:, c0:c0 + D_MODEL]) + b_ref[:, c0:c0 + D_MODEL]).astype(BF16)

    head_scale = HEAD_DIM ** -0.5
    for t in range(ATTN_W // LANES):
        qt = z_ref[:, t * LANES:(t + 1) * LANES]
        if latent:
            qt = _rope_tile(qt, cos_ref[...], sin_ref[...])
        q_ref[:, t * LANES:(t + 1) * LANES] = (qt * head_scale).astype(BF16)
    kt = z_ref[:, OFF_K:OFF_V]
    if latent:
        kt = _rope_tile(kt, cos_ref[...], sin_ref[...])
    k_ref[...] = kt
    v_ref[...] = z_ref[:, OFF_V:OFF_CA]
    g_ref[...] = z_ref[:, OFF_CA:OFF_CB] * _sigmoid(z_ref[:, OFF_CB:OFF_F])
    fcs_ref[...] = _dot(z_ref[:, OFF_F:OFF_G].astype(BF16), bcs_ref[...]).astype(BF16)


def _inproj(x, mod, mod_off, mod_stride, seq, w_in, b_in, bcs, rope, tm=256):
    n_tok = x.shape[0]
    latent = rope is not None
    tok = lambda i: (i, 0)
    in_specs = [pl.BlockSpec((tm, D_MODEL), tok),
                pl.BlockSpec((1, 6, D_MODEL), lambda i: (mod_off + mod_stride * ((i * tm) // seq), 0, 0)),
                _resident((D_MODEL, N_IN)),
                _resident((1, N_IN)),
                _resident((FNET_CH, 2 * FNET_CH))]
    args = [x, mod, w_in, b_in, bcs]
    if latent:
        pos = lambda i: (i % (seq // tm), 0)
        in_specs += [pl.BlockSpec((tm, LANES), pos), pl.BlockSpec((tm, LANES), pos)]
        args += list(rope)
    widths = (ATTN_W, KV_W, KV_W, CONV_CH, 2 * FNET_CH, N_GATES)
    dtypes = (BF16, F32, F32, F32, BF16, BF16)
    return pl.pallas_call(
        functools.partial(_inproj_kernel, latent=latent),
        grid=(n_tok // tm,),
        in_specs=in_specs,
        out_specs=[pl.BlockSpec((tm, w), tok) for w in widths],
        out_shape=[jax.ShapeDtypeStruct((n_tok, w), d) for w, d in zip(widths, dtypes)],
        scratch_shapes=[pltpu.VMEM((tm, D_MODEL), BF16), pltpu.VMEM((tm, OFF_G), F32)],
        compiler_params=_params(1),
        name="inproj_latent" if latent else "inproj_context",
    )(*args)


def _pair_halves(x, kv):
    lane = lax.broadcasted_iota(jnp.int32, x.shape, 1)
    low = lane < HEAD_DIM
    swapped = pltpu.roll(x, HEAD_DIM, 1)
    lo_src, hi_src = (x, swapped) if kv == 0 else (swapped, x)
    return jnp.where(low, lo_src, 0.0).astype(BF16), jnp.where(low, 0.0, hi_src).astype(BF16)


def _softmax_terms(s, sink):
    m = jnp.maximum(jnp.max(s, axis=-1, keepdims=True), sink)
    p = jnp.exp(s - m)
    denom = jnp.sum(p, axis=-1, keepdims=True) + jnp.exp(sink - m)
    return p, denom


def _attend_group(q_ref, o_ref, sink_ref, kv, kb, vb, mask):
    tq = q_ref.shape[0]
    n = kb.shape[0] // 2
    t0 = kv * (N_HEADS // N_KV_HEADS) // 2
    q2 = jnp.concatenate([q_ref[:, t0 * LANES:(t0 + 1) * LANES],
                          q_ref[:, (t0 + 1) * LANES:(t0 + 2) * LANES]], axis=0)
    s = lax.dot_general(q2, kb, (((1,), (1,)), ((), ())), preferred_element_type=F32)
    first_tile = lax.broadcasted_iota(jnp.int32, (2 * tq, 1), 0) < tq
    parts, inv = [], []
    for half in range(2):
        sh = s[:, half * n:(half + 1) * n]
        if mask is not None:
            nm = mask.shape[1]
            sh = jnp.concatenate([jnp.where(mask, sh[:, :nm], NEG_INF), sh[:, nm:]], axis=1)
        head = 2 * t0 + half
        sink = jnp.where(first_tile, sink_ref[head], sink_ref[head + 2])
        p, denom = _softmax_terms(sh, sink)
        parts.append(p.astype(BF16))
        inv.append(1.0 / denom)
    o = _dot(jnp.concatenate(parts, axis=1), vb)
    lane = lax.broadcasted_iota(jnp.int32, o.shape, 1)
    o = (o * jnp.where(lane < HEAD_DIM, inv[0], inv[1])).astype(o_ref.dtype)
    o_ref[:, t0 * LANES:(t0 + 1) * LANES] = o[:tq]
    o_ref[:, (t0 + 1) * LANES:(t0 + 2) * LANES] = o[tq:]


def _ctx_attn_kernel(sink_ref, q_ref, k_ref, v_ref, o_ref):
    for kv in range(N_KV_HEADS):
        kb = jnp.concatenate(_pair_halves(k_ref[...], kv), axis=0)
        vb = jnp.concatenate(_pair_halves(v_ref[...], kv), axis=0)
        _attend_group(q_ref, o_ref, sink_ref, kv, kb, vb, None)


def _ctx_attn(q, k, v, sink, seq):
    n_tok = q.shape[0]
    tok = lambda b: (b, 0)
    return pl.pallas_call(
        _ctx_attn_kernel,
        grid=(n_tok // seq,),
        in_specs=[pl.BlockSpec(memory_space=pltpu.SMEM),
                  pl.BlockSpec((seq, ATTN_W), tok),
                  pl.BlockSpec((seq, KV_W), tok),
                  pl.BlockSpec((seq, KV_W), tok)],
        out_specs=pl.BlockSpec((seq, ATTN_W), tok),
        out_shape=jax.ShapeDtypeStruct((n_tok, ATTN_W), BF16),
        compiler_params=_params(1),
        name="context_attention",
    )(sink, q, k, v)


def _win_attn_kernel(sink_ref, q_ref, k_ref, v_ref, ck_ref, cv_ref, o_ref,
                     kl_ref, vl_ref, kc_ref, vc_ref, *, seq):
    qb = pl.program_id(1)

    @pl.when(qb == 0)
    def _():
        for kv in range(N_KV_HEADS):
            for src, dst in ((k_ref[...], kl_ref), (v_ref[...], vl_ref),
                             (ck_ref[0, 0], kc_ref), (cv_ref[0, 0], vc_ref)):
                lo, hi = _pair_halves(src, kv)
                dst[kv, 0] = lo
                dst[kv, 1] = hi

    span = 3 * BLOCK
    start = pl.multiple_of(jnp.clip(qb * BLOCK - BLOCK, 0, seq - span), BLOCK)
    qpos = qb * BLOCK + lax.broadcasted_iota(jnp.int32, (BLOCK, span), 0)
    kpos = start + lax.broadcasted_iota(jnp.int32, (BLOCK, span), 1)
    mask = jnp.abs(qpos - kpos) <= WINDOW
    mask = jnp.concatenate([mask, mask], axis=0)
    for kv in range(N_KV_HEADS):
        kb = jnp.concatenate([kl_ref[kv, 0, pl.ds(start, span), :], kc_ref[kv, 0],
                              kl_ref[kv, 1, pl.ds(start, span), :], kc_ref[kv, 1]], axis=0)
        vb = jnp.concatenate([vl_ref[kv, 0, pl.ds(start, span), :], vc_ref[kv, 0],
                              vl_ref[kv, 1, pl.ds(start, span), :], vc_ref[kv, 1]], axis=0)
        _attend_group(q_ref, o_ref, sink_ref, kv, kb, vb, mask)


def _win_attn(q, k, v, cache_k, cache_v, layer, sink, seq):
    n_tok = q.shape[0]
    bsz = n_tok // seq
    nb = seq // BLOCK
    past = cache_k.shape[2]
    qmap = lambda b, i: (b * nb + i, 0)
    kvmap = lambda b, i: (b, 0)
    cmap = lambda b, i: (b, layer, 0, 0)
    return pl.pallas_call(
        functools.partial(_win_attn_kernel, seq=seq),
        grid=(bsz, nb),
        in_specs=[pl.BlockSpec(memory_space=pltpu.SMEM),
                  pl.BlockSpec((BLOCK, ATTN_W), qmap),
                  pl.BlockSpec((seq, KV_W), kvmap),
                  pl.BlockSpec((seq, KV_W), kvmap),
                  pl.BlockSpec((1, 1, past, KV_W), cmap),
                  pl.BlockSpec((1, 1, past, KV_W), cmap)],
        out_specs=pl.BlockSpec((BLOCK, ATTN_W), qmap),
        out_shape=jax.ShapeDtypeStruct((n_tok, ATTN_W), BF16),
        scratch_shapes=[pltpu.VMEM((N_KV_HEADS, 2, seq, KV_W), BF16)] * 2
        + [pltpu.VMEM((N_KV_HEADS, 2, past, KV_W), BF16)] * 2,
        compiler_params=_params(2),
        name="window_attention",
    )(sink, q, k, v, cache_k, cache_v)


CONV_ROWS = 32


def _mix_kernel(x_ref, mod_ref, attn_ref, g_ref, fcs_ref, cs_ref, gate_ref,
                wa_ref, wc_ref, wf_ref, wo_ref, cw_ref, cvec_ref, dvec_ref,
                o_ref, gp_ref, gs_ref, u_ref, m_ref, mb_ref, *, seq, tm):
    i = pl.program_id(1)
    nblk = seq // tm
    r0 = pl.multiple_of(i * tm, tm)

    gp_ref[CONV_PAD:CONV_PAD + tm, :] = g_ref[pl.ds(r0, tm), :]
    top = g_ref[pl.ds(pl.multiple_of(jnp.maximum(r0 - CONV_PAD, 0), SUBLANES), CONV_PAD), :]
    gp_ref[0:CONV_PAD, :] = jnp.where(i > 0, top, 0.0)
    bot = g_ref[pl.ds(pl.multiple_of(jnp.minimum(r0 + tm, seq - CONV_PAD), SUBLANES), CONV_PAD), :]
    gp_ref[CONV_PAD + tm:, :] = jnp.where(i < nblk - 1, bot, 0.0)
    n_pad = tm + 2 * CONV_PAD
    staged = gp_ref[...]
    for r in range(1, SUBLANES):
        gs_ref[r - 1] = pltpu.roll(staged, n_pad - r, 0)
    first = CONV_PAD - CONV_K // 2
    n_tile = CONV_ROWS // SUBLANES
    for rc in range(tm // CONV_ROWS):
        accs = [None] * n_tile
        for j in range(CONV_K):
            a, r = divmod(first + j, SUBLANES)
            src = gp_ref if r == 0 else gs_ref.at[r - 1]
            w8 = cw_ref[j]
            for t in range(n_tile):
                lo = rc * CONV_ROWS + (a + t) * SUBLANES
                term = w8 * src[lo:lo + SUBLANES, :]
                accs[t] = term if accs[t] is None else accs[t] + term
        acc = jnp.concatenate(accs, axis=0)
        y = _ln(acc + cvec_ref[0:1, :]) * cvec_ref[1:2, :] + cvec_ref[2:3, :]
        u_ref[rc * CONV_ROWS:(rc + 1) * CONV_ROWS, :] = (y * _sigmoid(y)).astype(BF16)

    fm = _dot(cs_ref[:, 0:seq], fcs_ref[:, 0:FNET_CH]) + _dot(cs_ref[:, seq:], fcs_ref[:, FNET_CH:])

    m_ref[...] = gate_ref[:, 0:D_MODEL].astype(F32) * _dot(attn_ref[...], wa_ref[...])
    m_ref[...] += gate_ref[:, D_MODEL:2 * D_MODEL].astype(F32) * _dot(u_ref[...], wc_ref[...])
    mb_ref[...] = (m_ref[...] + gate_ref[:, 2 * D_MODEL:].astype(F32)
                   * (_dot(fm.astype(BF16), wf_ref[...]) + dvec_ref[0:1, :])).astype(BF16)
    m_ref[...] = _dot(mb_ref[...], wo_ref[...])
    gain = mod_ref[0, 2:3, :]
    for rp in range(tm // ROW_PIECE):
        rows = slice(rp * ROW_PIECE, (rp + 1) * ROW_PIECE)
        r = ALPHA * x_ref[rows, :] + gain * m_ref[rows, :]
        o_ref[rows, :] = _ln(r) * dvec_ref[1:2, :] + dvec_ref[2:3, :]


def _mix(x, mod, mod_off, mod_stride, seq, attn, g, fcs, cs, gates, wa, wc, wf, wo, cw, cvec, dvec,
         tm=256):
    n_tok = x.shape[0]
    bsz = n_tok // seq
    nblk = seq // tm
    tok = lambda b, i: (b * nblk + i, 0)
    per_seq = lambda b, i: (b, 0)
    return pl.pallas_call(
        functools.partial(_mix_kernel, seq=seq, tm=tm),
        grid=(bsz, nblk),
        in_specs=[pl.BlockSpec((tm, D_MODEL), tok),
                  pl.BlockSpec((1, 6, D_MODEL), lambda b, i: (mod_off + mod_stride * b, 0, 0)),
                  pl.BlockSpec((tm, ATTN_W), tok),
                  pl.BlockSpec((seq, CONV_CH), per_seq),
                  pl.BlockSpec((seq, 2 * FNET_CH), per_seq),
                  pl.BlockSpec((tm, 2 * seq), lambda b, i: (i, 0)),
                  pl.BlockSpec((tm, N_GATES), tok),
                  _resident((ATTN_W, D_MODEL)),
                  _resident((CONV_CH, D_MODEL)),
                  _resident((FNET_CH, D_MODEL)),
                  _resident((D_MODEL, D_MODEL)),
                  _resident((CONV_K, SUBLANES, CONV_CH)),
                  _resident((3, CONV_CH)),
                  _resident((3, D_MODEL))],
        out_specs=pl.BlockSpec((tm, D_MODEL), tok),
        out_shape=jax.ShapeDtypeStruct((n_tok, D_MODEL), F32),
        scratch_shapes=[pltpu.VMEM((tm + 2 * CONV_PAD, CONV_CH), F32),
                        pltpu.VMEM((SUBLANES - 1, tm + 2 * CONV_PAD, CONV_CH), F32),
                        pltpu.VMEM((tm, CONV_CH), BF16),
                        pltpu.VMEM((tm, D_MODEL), F32),
                        pltpu.VMEM((tm, D_MODEL), BF16)],
        compiler_params=_params(2),
        name="mix_latent" if mod_stride else "mix_context",
    )(x, mod, attn, g, fcs, cs, gates, wa, wc, wf, wo, cw, cvec, dvec)


def _ffn_kernel(xp_ref, x_ref, xn_ref, mod_ref, wug_ref, wuv_ref, wd_ref, fw_ref, fb_ref, lvec_ref,
                o_ref, ext_ref, ug0_ref, uv0_ref, ug1_ref, uv1_ref, act_ref, *, seq, tm):
    i = pl.program_id(0)
    r0 = i * tm
    scale = 1.0 + mod_ref[0, 4:5, :]
    shift = mod_ref[0, 3:4, :]

    def modulated(x):
        return _ln(x) * scale + shift

    hp = jnp.where(r0 % seq == 0, 0.0, modulated(xp_ref[...]))
    hn = jnp.where((r0 + tm) % seq == 0, 0.0, modulated(xn_ref[...]))
    ext_ref[...] = jnp.concatenate([hp, modulated(x_ref[...]), hn], axis=0).astype(BF16)

    n_chunk = D_FF // FFN_CHUNK
    u_refs = ((ug0_ref, uv0_ref), (ug1_ref, uv1_ref))
    piece = FFN_ROWS + 2 * SUBLANES

    def up(c):
        g_ref, v_ref = u_refs[c % 2]
        lo = c * FFN_CHUNK
        g_ref[...] = _dot(ext_ref[...], wug_ref[:, lo:lo + FFN_CHUNK])
        v_ref[...] = _dot(ext_ref[...], wuv_ref[:, lo:lo + FFN_CHUNK])

    def conv3(u_ref, p, col):
        lo = col * FFN_CHUNK
        blk = u_ref[p * FFN_ROWS:p * FFN_ROWS + piece, :]
        prev = pltpu.roll(blk, 1, 0)[SUBLANES:SUBLANES + FFN_ROWS]
        nxt = pltpu.roll(blk, piece - 1, 0)[SUBLANES:SUBLANES + FFN_ROWS]
        return (fw_ref[0:1, lo:lo + FFN_CHUNK] * prev
                + fw_ref[1:2, lo:lo + FFN_CHUNK] * blk[SUBLANES:SUBLANES + FFN_ROWS]
                + fw_ref[2:3, lo:lo + FFN_CHUNK] * nxt + fb_ref[:, lo:lo + FFN_CHUNK])

    def activate(c):
        g_ref, v_ref = u_refs[c % 2]
        for p in range(tm // FFN_ROWS):
            yg = conv3(g_ref, p, c)
            yv = conv3(v_ref, p, c + n_chunk)
            act_ref[p * FFN_ROWS:(p + 1) * FFN_ROWS, c * FFN_CHUNK:(c + 1) * FFN_CHUNK] = (
                yg * _sigmoid(yg) * yv).astype(BF16)

    up(0)
    for c in range(n_chunk):
        if c + 1 < n_chunk:
            up(c + 1)
        activate(c)
    y = _dot(act_ref[...], wd_ref[...])
    r = ALPHA * x_ref[...] + mod_ref[0, 5:6, :] * y
    o_ref[...] = _ln(r) * lvec_ref[0:1, :] + lvec_ref[1:2, :]


def _ffn(x, mod, mod_off, mod_stride, seq, wug, wuv, wd, fw, fb, lvec):
    n_tok = x.shape[0]
    tm = min(512, seq)
    assert seq % tm == 0
    halo = tm // SUBLANES
    last = n_tok // SUBLANES - 1
    n_ext = tm + 2 * SUBLANES
    return pl.pallas_call(
        functools.partial(_ffn_kernel, seq=seq, tm=tm),
        grid=(n_tok // tm,),
        in_specs=[pl.BlockSpec((SUBLANES, D_MODEL), lambda i: (jnp.maximum(i * halo - 1, 0), 0)),
                  pl.BlockSpec((tm, D_MODEL), lambda i: (i, 0)),
                  pl.BlockSpec((SUBLANES, D_MODEL), lambda i: (jnp.minimum((i + 1) * halo, last), 0)),
                  pl.BlockSpec((1, 6, D_MODEL), lambda i: (mod_off + mod_stride * ((i * tm) // seq), 0, 0)),
                  _resident((D_MODEL, D_FF)),
                  _resident((D_MODEL, D_FF)),
                  _resident((D_FF, D_MODEL)),
                  _resident((3, 2 * D_FF)),
                  _resident((1, 2 * D_FF)),
                  _resident((2, D_MODEL))],
        out_specs=pl.BlockSpec((tm, D_MODEL), lambda i: (i, 0)),
        out_shape=jax.ShapeDtypeStruct((n_tok, D_MODEL), F32),
        scratch_shapes=[pltpu.VMEM((n_ext, D_MODEL), BF16)]
        + [pltpu.VMEM((n_ext, FFN_CHUNK), F32)] * 4
        + [pltpu.VMEM((tm, D_FF), BF16)],
        compiler_params=_params(1),
        name="ffn_latent" if mod_stride else "ffn_context",
    )(x, x, x, mod, wug, wuv, wd, fw, fb, lvec)


def _dft_cos_sin(n):
    idx = np.arange(n, dtype=np.int64)
    ang = 2.0 * np.pi * ((idx[:, None] * idx[None, :]) % n) / n
    return np.cos(ang) / np.sqrt(n), np.sin(ang) / np.sqrt(n)


def _channel_dft_table():
    c, s = _dft_cos_sin(FNET_GC)
    out = np.zeros((FNET_CH, 2 * FNET_CH), np.float32)
    for gi in range(FNET_GROUPS):
        sl = slice(gi * FNET_GC, (gi + 1) * FNET_GC)
        out[sl, sl] = c
        out[sl, FNET_CH + gi * FNET_GC:FNET_CH + (gi + 1) * FNET_GC] = s
    return jnp.asarray(out, dtype=BF16)


def _position_dft_table(n):
    c, s = _dft_cos_sin(n)
    return jnp.asarray(np.concatenate([c, -s], axis=1).astype(np.float32), dtype=BF16)


def _rope_tables(length):
    n_rows = length // GRID_W
    row = jnp.repeat(jnp.arange(n_rows, dtype=F32), GRID_W)
    col = jnp.tile(jnp.arange(GRID_W, dtype=F32), n_rows)
    quarter = HEAD_DIM // 4
    inv = ROPE_THETA ** (-jnp.arange(quarter, dtype=F32) / quarter)
    ang = jnp.concatenate([row[:, None] * inv, col[:, None] * inv], axis=-1)
    cos, sin = jnp.cos(ang), jnp.sin(ang)
    reps = LANES // HEAD_DIM
    return (jnp.tile(jnp.concatenate([cos, cos], axis=-1), (1, reps)),
            jnp.tile(jnp.concatenate([-sin, sin], axis=-1), (1, reps)))


def kernel(x_prompt, x_sample, cache_k, cache_v, c, c_ctx, w_mod, b_mod, w_in, b_in, sink,
           w_attn_o, conv_w, conv_b, conv_ln_g, conv_ln_b, w_conv_o, w_fnet, b_fnet, w_o,
           ln1_g, ln1_b, w_up, ffn_conv_w, ffn_conv_b, w_down, ln2_g, ln2_b):
    batch, seq_p, _ = x_prompt.shape
    dec_batch, seq_s, _ = x_sample.shape
    past = cache_k.shape[2]
    assert 1 + dec_batch <= MOD_ROWS

    cc = jnp.concatenate([c_ctx[None, :], c], axis=0)
    cc = jnp.pad(cc, ((0, MOD_ROWS - cc.shape[0]), (0, 0)))
    mods = _modulation(cc, w_mod, b_mod).reshape(DEPTH, MOD_ROWS, 6, D_MODEL)

    bcs = _channel_dft_table()
    cs_p = _position_dft_table(seq_p)
    cs_s = _position_dft_table(seq_s)
    rope = _rope_tables(seq_s)
    ck = cache_k.reshape(dec_batch, DEPTH, past, KV_W)
    cv = cache_v.reshape(dec_batch, DEPTH, past, KV_W)

    xp = x_prompt.reshape(batch * seq_p, D_MODEL)
    xs = x_sample.reshape(dec_batch * seq_s, D_MODEL)
    new_k, new_v = [], []
    for l in range(DEPTH):
        mod = mods[l]
        w_in_l = w_in[l].astype(BF16)
        b_in_l = b_in[l][None, :]
        wa, wc, wf, wo = (w.astype(BF16) for w in (w_attn_o[l], w_conv_o[l], w_fnet[l], w_o[l]))
        cvec = jnp.stack([conv_b[l], conv_ln_g[l], conv_ln_b[l]])
        dvec = jnp.stack([b_fnet[l], ln1_g[l], ln1_b[l]])
        wug = w_up[l, :, :D_FF].astype(BF16)
        wuv = w_up[l, :, D_FF:].astype(BF16)
        wd = w_down[l].astype(BF16)
        fb = ffn_conv_b[l][None, :]
        lvec = jnp.stack([ln2_g[l], ln2_b[l]])

        def run(x, mod_off, mod_stride, seq, rope_tabs, attend, cs):
            q, k, v, g, fcs, gates = _inproj(x, mod, mod_off, mod_stride, seq, w_in_l, b_in_l, bcs,
                                             rope_tabs)
            attn = attend(q, k, v)
            x = _mix(x, mod, mod_off, mod_stride, seq, attn, g, fcs, cs, gates, wa, wc, wf, wo,
                     jnp.broadcast_to(conv_w[l][:, None, :], (CONV_K, SUBLANES, CONV_CH)), cvec, dvec)
            x = _ffn(x, mod, mod_off, mod_stride, seq, wug, wuv, wd, ffn_conv_w[l], fb, lvec)
            return x, k, v

        xp, kp, vp = run(xp, 0, 0, seq_p, None,
                         lambda q, k, v: _ctx_attn(q, k, v, sink[l], seq_p), cs_p)
        new_k.append(kp.reshape(batch, seq_p, N_KV_HEADS, HEAD_DIM))
        new_v.append(vp.reshape(batch, seq_p, N_KV_HEADS, HEAD_DIM))
        xs, _, _ = run(xs, 1, 1, seq_s, rope,
                       lambda q, k, v: _win_attn(q, k, v, ck, cv, l, sink[l], seq_s), cs_s)

    return (xp.reshape(batch, seq_p, D_MODEL), xs.reshape(dec_batch, seq_s, D_MODEL),
            jnp.stack(new_k, axis=1), jnp.stack(new_v, axis=1))
```

```python
import functools
import math

import jax
import jax.numpy as jnp
import numpy as np
from jax import lax
from jax.experimental import pallas as pl
from jax.experimental.pallas import tpu as pltpu

D_MODEL = 1024
DEPTH = 2
GRID_W = 64
N_HEADS = 8
N_KV_HEADS = 2
HEAD_DIM = 64
ATTN_W = N_HEADS * HEAD_DIM
KV_W = N_KV_HEADS * HEAD_DIM
WINDOW = 128
BLOCK = 128
CONV_CH = D_MODEL // 4
CONV_K = 31
CONV_PAD = 16
FNET_GROUPS = 4
FNET_CH = D_MODEL // 4
FNET_GC = FNET_CH // FNET_GROUPS
N_GATES = 3 * D_MODEL
N_IN = ATTN_W + 2 * KV_W + 2 * CONV_CH + FNET_CH + N_GATES
D_FF = int(math.ceil(8 * D_MODEL / 3 / 128)) * 128
FFN_CHUNK = 256
FFN_ROWS = 48
ROPE_THETA = 10000.0
ALPHA = (2 * DEPTH) ** 0.25
NEG_INF = -1e30
LN_EPS = 1e-6

OFF_K = ATTN_W
OFF_V = OFF_K + KV_W
OFF_CA = OFF_V + KV_W
OFF_CB = OFF_CA + CONV_CH
OFF_F = OFF_CB + CONV_CH
OFF_G = OFF_F + FNET_CH

LANES = 128
SUBLANES = 8
VMEM_LIMIT = 56 * 1024 * 1024
MOD_ROWS = 16
ROW_PIECE = 32

F32 = jnp.float32
BF16 = jnp.bfloat16


def _resident(shape, layer=None):
    nd = len(shape)
    if layer is None:
        return pl.BlockSpec(shape, lambda *_: (0,) * nd, pipeline_mode=pl.Buffered(1))
    return pl.BlockSpec((None,) + tuple(shape), lambda *_: (layer,) + (0,) * nd,
                        pipeline_mode=pl.Buffered(1))


def _mod_spec(layer, row_of):
    return pl.BlockSpec((None, 1, 6, D_MODEL), lambda *idx: (layer, row_of(*idx), 0, 0))


def _params(n_axes):
    return pltpu.CompilerParams(dimension_semantics=("arbitrary",) * n_axes,
                                vmem_limit_bytes=VMEM_LIMIT)


def _sigmoid(x):
    return 1.0 / (1.0 + jnp.exp(-x))


def _ln(x):
    mu = jnp.mean(x, axis=-1, keepdims=True)
    xc = x - mu
    var = jnp.mean(xc * xc, axis=-1, keepdims=True)
    return xc * lax.rsqrt(var + LN_EPS)


def _dot(a, b):
    return jnp.dot(a, b, preferred_element_type=F32)


def _mod_kernel(c_ref, w_ref, b_ref, o_ref):
    c = c_ref[...]
    a = (c * _sigmoid(c)).astype(BF16)
    o_ref[0] = _dot(a, w_ref[0].astype(BF16)) + b_ref[0]


def _modulation(cc, w_mod, b_mod):
    tn = 1536
    n_out = w_mod.shape[-1]
    return pl.pallas_call(
        _mod_kernel,
        grid=(DEPTH, n_out // tn),
        in_specs=[pl.BlockSpec((MOD_ROWS, D_MODEL), lambda l, j: (0, 0)),
                  pl.BlockSpec((1, D_MODEL, tn), lambda l, j: (l, 0, j)),
                  pl.BlockSpec((1, 1, tn), lambda l, j: (l, 0, j))],
        out_specs=pl.BlockSpec((1, MOD_ROWS, tn), lambda l, j: (l, 0, j)),
        out_shape=jax.ShapeDtypeStruct((DEPTH, MOD_ROWS, n_out), F32),
        compiler_params=_params(2),
        name="modulation",
    )(cc, w_mod, b_mod.reshape(DEPTH, 1, n_out))


def _rope_tile(t, cos, sin):
    lane = lax.broadcasted_iota(jnp.int32, t.shape, 1)
    first_half = (lane % HEAD_DIM) < (HEAD_DIM // 2)
    partner = jnp.where(first_half, pltpu.roll(t, LANES - HEAD_DIM // 2, 1),
                        pltpu.roll(t, HEAD_DIM // 2, 1))
    return t * cos + partner * sin


def _inproj_kernel(*refs, latent):
    if latent:
        (x_ref, mod_ref, w_ref, b_ref, bcs_ref, cos_ref, sin_ref,
         q_ref, k_ref, v_ref, g_ref, fcs_ref, gate_ref, hb_ref, z_ref) = refs
    else:
        (x_ref, mod_ref, w_ref, b_ref, bcs_ref,
         q_ref, k_ref, v_ref, g_ref, fcs_ref, gate_ref, hb_ref, z_ref) = refs
    tm = x_ref.shape[0]
    scale = 1.0 + mod_ref[0, 1:2, :]
    shift = mod_ref[0, 0:1, :]
    for rp in range(tm // ROW_PIECE):
        rows = slice(rp * ROW_PIECE, (rp + 1) * ROW_PIECE)
        hb_ref[rows, :] = (_ln(x_ref[rows, :]) * scale + shift).astype(BF16)

    z_ref[...] = _dot(hb_ref[...], w_ref[:, 0:OFF_G]) + b_ref[:, 0:OFF_G]
    for t in range(3):
        c0 = OFF_G + t * D_MODEL
        gate_ref[:, t * D_MODEL:(t + 1) * D_MODEL] = _sigmoid(
            _dot(hb_ref[...], w_ref[:, c0:c0 + D_MODEL]) + b_ref[:, c0:c0 + D_MODEL]).astype(BF16)

    head_scale = HEAD_DIM ** -0.5
    for t in range(ATTN_W // LANES):
        qt = z_ref[:, t * LANES:(t + 1) * LANES]
        if latent:
            qt = _rope_tile(qt, cos_ref[...], sin_ref[...])
        q_ref[:, t * LANES:(t + 1) * LANES] = (qt * head_scale).astype(BF16)
    kt = z_ref[:, OFF_K:OFF_V]
    if latent:
        kt = _rope_tile(kt, cos_ref[...], sin_ref[...])
    k_ref[...] = kt
    v_ref[...] = z_ref[:, OFF_V:OFF_CA]
    g_ref[...] = z_ref[:, OFF_CA:OFF_CB] * _sigmoid(z_ref[:, OFF_CB:OFF_F])
    fcs_ref[...] = _dot(z_ref[:, OFF_F:OFF_G].astype(BF16), bcs_ref[...]).astype(BF16)


def _inproj(x, mod, layer, mod_off, mod_stride, seq, w_in, b_in, bcs, rope, tm=256):
    n_tok = x.shape[0]
    latent = rope is not None
    tok = lambda i: (i, 0)
    in_specs = [pl.BlockSpec((tm, D_MODEL), tok),
                _mod_spec(layer, lambda i: mod_off + mod_stride * ((i * tm) // seq)),
                _resident((D_MODEL, N_IN), layer),
                _resident((1, N_IN), layer),
                _resident((FNET_CH, 2 * FNET_CH))]
    args = [x, mod, w_in, b_in, bcs]
    if latent:
        pos = lambda i: (i % (seq // tm), 0)
        in_specs += [pl.BlockSpec((tm, LANES), pos), pl.BlockSpec((tm, LANES), pos)]
        args += list(rope)
    widths = (ATTN_W, KV_W, KV_W, CONV_CH, 2 * FNET_CH, N_GATES)
    dtypes = (BF16, F32, F32, F32, BF16, BF16)
    return pl.pallas_call(
        functools.partial(_inproj_kernel, latent=latent),
        grid=(n_tok // tm,),
        in_specs=in_specs,
        out_specs=[pl.BlockSpec((tm, w), tok) for w in widths],
        out_shape=[jax.ShapeDtypeStruct((n_tok, w), d) for w, d in zip(widths, dtypes)],
        scratch_shapes=[pltpu.VMEM((tm, D_MODEL), BF16), pltpu.VMEM((tm, OFF_G), F32)],
        compiler_params=_params(1),
        name="inproj_latent" if latent else "inproj_context",
    )(*args)


def _pair_halves(x, kv):
    lane = lax.broadcasted_iota(jnp.int32, x.shape, 1)
    low = lane < HEAD_DIM
    swapped = pltpu.roll(x, HEAD_DIM, 1)
    lo_src, hi_src = (x, swapped) if kv == 0 else (swapped, x)
    return jnp.where(low, lo_src, 0.0).astype(BF16), jnp.where(low, 0.0, hi_src).astype(BF16)


def _softmax_terms(s, sink):
    m = jnp.maximum(jnp.max(s, axis=-1, keepdims=True), sink)
    p = jnp.exp(s - m)
    denom = jnp.sum(p, axis=-1, keepdims=True) + jnp.exp(sink - m)
    return p, denom


def _attend_group(q_ref, o_ref, sink_ref, kv, kb, vb, mask):
    tq = q_ref.shape[0]
    n = kb.shape[0] // 2
    t0 = kv * (N_HEADS // N_KV_HEADS) // 2
    q2 = jnp.concatenate([q_ref[:, t0 * LANES:(t0 + 1) * LANES],
                          q_ref[:, (t0 + 1) * LANES:(t0 + 2) * LANES]], axis=0)
    s = lax.dot_general(q2, kb, (((1,), (1,)), ((), ())), preferred_element_type=F32)
    first_tile = lax.broadcasted_iota(jnp.int32, (2 * tq, 1), 0) < tq
    parts, inv = [], []
    for half in range(2):
        sh = s[:, half * n:(half + 1) * n]
        if mask is not None:
            nm = mask.shape[1]
            sh = jnp.concatenate([jnp.where(mask, sh[:, :nm], NEG_INF), sh[:, nm:]], axis=1)
        head = 2 * t0 + half
        sink = jnp.where(first_tile, sink_ref[head], sink_ref[head + 2])
        p, denom = _softmax_terms(sh, sink)
        parts.append(p.astype(BF16))
        inv.append(1.0 / denom)
    o = _dot(jnp.concatenate(parts, axis=1), vb)
    lane = lax.broadcasted_iota(jnp.int32, o.shape, 1)
    o = (o * jnp.where(lane < HEAD_DIM, inv[0], inv[1])).astype(o_ref.dtype)
    o_ref[:, t0 * LANES:(t0 + 1) * LANES] = o[:tq]
    o_ref[:, (t0 + 1) * LANES:(t0 + 2) * LANES] = o[tq:]


def _ctx_attn_kernel(sink_ref, q_ref, k_ref, v_ref, o_ref):
    for kv in range(N_KV_HEADS):
        kb = jnp.concatenate(_pair_halves(k_ref[...], kv), axis=0)
        vb = jnp.concatenate(_pair_halves(v_ref[...], kv), axis=0)
        _attend_group(q_ref, o_ref, sink_ref, kv, kb, vb, None)


def _ctx_attn(q, k, v, sink, seq):
    n_tok = q.shape[0]
    tok = lambda b: (b, 0)
    return pl.pallas_call(
        _ctx_attn_kernel,
        grid=(n_tok // seq,),
        in_specs=[pl.BlockSpec(memory_space=pltpu.SMEM),
                  pl.BlockSpec((seq, ATTN_W), tok),
                  pl.BlockSpec((seq, KV_W), tok),
                  pl.BlockSpec((seq, KV_W), tok)],
        out_specs=pl.BlockSpec((seq, ATTN_W), tok),
        out_shape=jax.ShapeDtypeStruct((n_tok, ATTN_W), BF16),
        compiler_params=_params(1),
        name="context_attention",
    )(sink, q, k, v)


def _win_attn_kernel(sink_ref, q_ref, k_ref, v_ref, ck_ref, cv_ref, o_ref,
                     kl_ref, vl_ref, kc_ref, vc_ref, *, seq):
    qb = pl.program_id(1)

    @pl.when(qb == 0)
    def _():
        for kv in range(N_KV_HEADS):
            for src, dst in ((k_ref[...], kl_ref), (v_ref[...], vl_ref),
                             (ck_ref[0, 0], kc_ref), (cv_ref[0, 0], vc_ref)):
                lo, hi = _pair_halves(src, kv)
                dst[kv, 0] = lo
                dst[kv, 1] = hi

    span = 3 * BLOCK
    start = pl.multiple_of(jnp.clip(qb * BLOCK - BLOCK, 0, seq - span), BLOCK)
    qpos = qb * BLOCK + lax.broadcasted_iota(jnp.int32, (BLOCK, span), 0)
    kpos = start + lax.broadcasted_iota(jnp.int32, (BLOCK, span), 1)
    mask = jnp.abs(qpos - kpos) <= WINDOW
    mask = jnp.concatenate([mask, mask], axis=0)
    for kv in range(N_KV_HEADS):
        kb = jnp.concatenate([kl_ref[kv, 0, pl.ds(start, span), :], kc_ref[kv, 0],
                              kl_ref[kv, 1, pl.ds(start, span), :], kc_ref[kv, 1]], axis=0)
        vb = jnp.concatenate([vl_ref[kv, 0, pl.ds(start, span), :], vc_ref[kv, 0],
                              vl_ref[kv, 1, pl.ds(start, span), :], vc_ref[kv, 1]], axis=0)
        _attend_group(q_ref, o_ref, sink_ref, kv, kb, vb, mask)


def _win_attn(q, k, v, cache_k, cache_v, layer, sink, seq):
    n_tok = q.shape[0]
    bsz = n_tok // seq
    nb = seq // BLOCK
    past = cache_k.shape[2]
    qmap = lambda b, i: (b * nb + i, 0)
    kvmap = lambda b, i: (b, 0)
    cmap = lambda b, i: (b, layer, 0, 0)
    return pl.pallas_call(
        functools.partial(_win_attn_kernel, seq=seq),
        grid=(bsz, nb),
        in_specs=[pl.BlockSpec(memory_space=pltpu.SMEM),
                  pl.BlockSpec((BLOCK, ATTN_W), qmap),
                  pl.BlockSpec((seq, KV_W), kvmap),
                  pl.BlockSpec((seq, KV_W), kvmap),
                  pl.BlockSpec((1, 1, past, KV_W), cmap),
                  pl.BlockSpec((1, 1, past, KV_W), cmap)],
        out_specs=pl.BlockSpec((BLOCK, ATTN_W), qmap),
        out_shape=jax.ShapeDtypeStruct((n_tok, ATTN_W), BF16),
        scratch_shapes=[pltpu.VMEM((N_KV_HEADS, 2, seq, KV_W), BF16)] * 2
        + [pltpu.VMEM((N_KV_HEADS, 2, past, KV_W), BF16)] * 2,
        compiler_params=_params(2),
        name="window_attention",
    )(sink, q, k, v, cache_k, cache_v)


CONV_ROWS = 32


def _mix_kernel(x_ref, mod_ref, attn_ref, g_ref, fcs_ref, cs_ref, gate_ref,
                wa_ref, wc_ref, wf_ref, wo_ref, cw_ref, cvec_ref, dvec_ref,
                o_ref, gp_ref, gs_ref, u_ref, m_ref, mb_ref, *, seq, tm):
    i = pl.program_id(1)
    nblk = seq // tm
    r0 = pl.multiple_of(i * tm, tm)

    gp_ref[CONV_PAD:CONV_PAD + tm, :] = g_ref[pl.ds(r0, tm), :]
    top = g_ref[pl.ds(pl.multiple_of(jnp.maximum(r0 - CONV_PAD, 0), SUBLANES), CONV_PAD), :]
    gp_ref[0:CONV_PAD, :] = jnp.where(i > 0, top, 0.0)
    bot = g_ref[pl.ds(pl.multiple_of(jnp.minimum(r0 + tm, seq - CONV_PAD), SUBLANES), CONV_PAD), :]
    gp_ref[CONV_PAD + tm:, :] = jnp.where(i < nblk - 1, bot, 0.0)
    n_pad = tm + 2 * CONV_PAD
    staged = gp_ref[...]
    for r in range(1, SUBLANES):
        gs_ref[r - 1] = pltpu.roll(staged, n_pad - r, 0)
    first = CONV_PAD - CONV_K // 2
    n_tile = CONV_ROWS // SUBLANES
    for rc in range(tm // CONV_ROWS):
        accs = [None] * n_tile
        for j in range(CONV_K):
            a, r = divmod(first + j, SUBLANES)
            src = gp_ref if r == 0 else gs_ref.at[r - 1]
            w8 = cw_ref[j]
            for t in range(n_tile):
                lo = rc * CONV_ROWS + (a + t) * SUBLANES
                term = w8 * src[lo:lo + SUBLANES, :]
                accs[t] = term if accs[t] is None else accs[t] + term
        acc = jnp.concatenate(accs, axis=0)
        y = _ln(acc + cvec_ref[0:1, :]) * cvec_ref[1:2, :] + cvec_ref[2:3, :]
        u_ref[rc * CONV_ROWS:(rc + 1) * CONV_ROWS, :] = (y * _sigmoid(y)).astype(BF16)

    fm = (_dot(cs_ref[pl.ds(r0, tm), 0:seq], fcs_ref[:, 0:FNET_CH])
          + _dot(cs_ref[pl.ds(r0, tm), seq:], fcs_ref[:, FNET_CH:]))

    m_ref[...] = gate_ref[:, 0:D_MODEL].astype(F32) * _dot(attn_ref[...], wa_ref[...])
    m_ref[...] += gate_ref[:, D_MODEL:2 * D_MODEL].astype(F32) * _dot(u_ref[...], wc_ref[...])
    mb_ref[...] = (m_ref[...] + gate_ref[:, 2 * D_MODEL:].astype(F32)
                   * (_dot(fm.astype(BF16), wf_ref[...]) + dvec_ref[0:1, :])).astype(BF16)
    m_ref[...] = _dot(mb_ref[...], wo_ref[...])
    gain = mod_ref[0, 2:3, :]
    for rp in range(tm // ROW_PIECE):
        rows = slice(rp * ROW_PIECE, (rp + 1) * ROW_PIECE)
        r = ALPHA * x_ref[rows, :] + gain * m_ref[rows, :]
        o_ref[rows, :] = _ln(r) * dvec_ref[1:2, :] + dvec_ref[2:3, :]


def _mix(x, mod, layer, mod_off, mod_stride, seq, attn, g, fcs, cs, gates, wa, wc, wf, wo, cw, cvec,
         dvec, tm=256):
    n_tok = x.shape[0]
    bsz = n_tok // seq
    nblk = seq // tm
    tok = lambda b, i: (b * nblk + i, 0)
    per_seq = lambda b, i: (b, 0)
    return pl.pallas_call(
        functools.partial(_mix_kernel, seq=seq, tm=tm),
        grid=(bsz, nblk),
        in_specs=[pl.BlockSpec((tm, D_MODEL), tok),
                  _mod_spec(layer, lambda b, i: mod_off + mod_stride * b),
                  pl.BlockSpec((tm, ATTN_W), tok),
                  pl.BlockSpec((seq, CONV_CH), per_seq),
                  pl.BlockSpec((seq, 2 * FNET_CH), per_seq),
                  _resident((seq, 2 * seq)),
                  pl.BlockSpec((tm, N_GATES), tok),
                  _resident((ATTN_W, D_MODEL), layer),
                  _resident((CONV_CH, D_MODEL), layer),
                  _resident((FNET_CH, D_MODEL), layer),
                  _resident((D_MODEL, D_MODEL), layer),
                  _resident((CONV_K, SUBLANES, CONV_CH), layer),
                  _resident((3, CONV_CH), layer),
                  _resident((3, D_MODEL), layer)],
        out_specs=pl.BlockSpec((tm, D_MODEL), tok),
        out_shape=jax.ShapeDtypeStruct((n_tok, D_MODEL), F32),
        scratch_shapes=[pltpu.VMEM((tm + 2 * CONV_PAD, CONV_CH), F32),
                        pltpu.VMEM((SUBLANES - 1, tm + 2 * CONV_PAD, CONV_CH), F32),
                        pltpu.VMEM((tm, CONV_CH), BF16),
                        pltpu.VMEM((tm, D_MODEL), F32),
                        pltpu.VMEM((tm, D_MODEL), BF16)],
        compiler_params=_params(2),
        name="mix_latent" if mod_stride else "mix_context",
    )(x, mod, attn, g, fcs, cs, gates, wa, wc, wf, wo, cw, cvec, dvec)


def _ffn_kernel(xp_ref, x_ref, xn_ref, mod_ref, wug_ref, wuv_ref, wd_ref, fw_ref, fb_ref, lvec_ref,
                o_ref, xs_ref, ext_ref, ug0_ref, uv0_ref, ug1_ref, uv1_ref, act_ref, y_ref, ys_ref,
                *, seq, tm):
    i = pl.program_id(0)
    r0 = i * tm
    n_slab = D_MODEL // LANES
    pitch = tm // SUBLANES + 2
    n_virt = SUBLANES * pitch
    scale = 1.0 + mod_ref[0, 4:5, :]
    shift = mod_ref[0, 3:4, :]
    gain = mod_ref[0, 5:6, :]
    n_chunk = D_FF // FFN_CHUNK
    row8 = lax.broadcasted_iota(jnp.int32, (SUBLANES, D_MODEL), 0)
    row16 = lax.broadcasted_iota(jnp.int32, (2 * SUBLANES, 1), 0)
    pieces = [(lo, min(lo + FFN_ROWS, n_virt)) for lo in range(0, n_virt, FFN_ROWS)]

    tail = jnp.concatenate([jnp.where(row8 == 0, xn_ref[...], 0.0),
                            jnp.where(row8 == SUBLANES - 1, xp_ref[...], 0.0)], axis=0)
    for c in range(n_slab):
        xs_ref[c, 0:tm, :] = x_ref[:, c * LANES:(c + 1) * LANES]
        xs_ref[c, tm:n_virt, :] = tail[:, c * LANES:(c + 1) * LANES]

    def strided_rows(j):
        return jnp.concatenate(
            [xs_ref[c, pl.ds(j, SUBLANES, stride=pitch), :] for c in range(n_slab)], axis=1)

    j_next = tm - (SUBLANES - 1) * pitch
    j_prev = pitch - 1
    no_next = (r0 + tm) % seq == 0
    no_prev = r0 % seq == 0
    for jj in range(0, pitch, 2):
        xj = jnp.concatenate([strided_rows(jj), strided_rows(jj + 1)], axis=0)
        h = _ln(xj) * scale + shift
        for j_halo, absent in ((j_next, no_next), (j_prev, no_prev)):
            if jj <= j_halo < jj + 2:
                halo_row = (j_halo - jj) * SUBLANES + SUBLANES - 1
                h = jnp.where((row16 == halo_row) & absent, 0.0, h)
        ext_ref[jj * SUBLANES:(jj + 2) * SUBLANES, :] = h.astype(BF16)

    u_refs = ((ug0_ref, uv0_ref), (ug1_ref, uv1_ref))

    def up(c):
        g_ref, v_ref = u_refs[c % 2]
        lo = c * FFN_CHUNK
        g_ref[...] = _dot(ext_ref[...], wug_ref[:, lo:lo + FFN_CHUNK])
        v_ref[...] = _dot(ext_ref[...], wuv_ref[:, lo:lo + FFN_CHUNK])

    def conv3(u_ref, lo, hi, col):
        c0 = col * FFN_CHUNK
        if lo >= SUBLANES:
            prev = u_ref[lo - SUBLANES:hi - SUBLANES, :]
        else:
            prev = jnp.concatenate([pltpu.roll(u_ref[n_virt - SUBLANES:n_virt, :], 1, 0),
                                    u_ref[0:hi - SUBLANES, :]], axis=0)
        if hi + SUBLANES <= n_virt:
            nxt = u_ref[lo + SUBLANES:hi + SUBLANES, :]
        else:
            nxt = jnp.concatenate([u_ref[lo + SUBLANES:n_virt, :],
                                   pltpu.roll(u_ref[0:SUBLANES, :], SUBLANES - 1, 0)], axis=0)

        def rows(w8):
            return jnp.concatenate([w8] * ((hi - lo) // SUBLANES), axis=0)

        cols = slice(c0, c0 + FFN_CHUNK)
        return (rows(fw_ref[0, :, cols]) * prev + rows(fw_ref[1, :, cols]) * u_ref[lo:hi, :]
                + rows(fw_ref[2, :, cols]) * nxt + rows(fb_ref[:, cols]))

    def activate(c):
        g_ref, v_ref = u_refs[c % 2]
        for lo, hi in pieces:
            yg = conv3(g_ref, lo, hi, c)
            yv = conv3(v_ref, lo, hi, c + n_chunk)
            act_ref[lo:hi, c * FFN_CHUNK:(c + 1) * FFN_CHUNK] = (yg * _sigmoid(yg) * yv).astype(BF16)

    up(0)
    for c in range(n_chunk):
        if c + 1 < n_chunk:
            up(c + 1)
        activate(c)
    y_ref[...] = _dot(act_ref[...], wd_ref[...])

    for jj in range(0, pitch, 2):
        xj = jnp.concatenate([strided_rows(jj), strided_rows(jj + 1)], axis=0)
        r = ALPHA * xj + gain * y_ref[jj * SUBLANES:(jj + 2) * SUBLANES, :]
        out = _ln(r) * lvec_ref[0:1, :] + lvec_ref[1:2, :]
        for k in range(2):
            for c in range(n_slab):
                ys_ref[c, pl.ds(jj + k, SUBLANES, stride=pitch), :] = (
                    out[k * SUBLANES:(k + 1) * SUBLANES, c * LANES:(c + 1) * LANES])
    for c in range(n_slab):
        o_ref[:, c * LANES:(c + 1) * LANES] = ys_ref[c, 0:tm, :]


def _ffn(x, mod, layer, mod_off, mod_stride, seq, w_up, wd, fw, fb, lvec):
    n_tok = x.shape[0]
    tm = min(512, seq)
    assert seq % tm == 0
    halo = tm // SUBLANES
    last = n_tok // SUBLANES - 1
    n_virt = tm + 2 * SUBLANES
    half = lambda j: pl.BlockSpec((None, D_MODEL, D_FF), lambda i: (layer, 0, j),
                                  pipeline_mode=pl.Buffered(1))
    return pl.pallas_call(
        functools.partial(_ffn_kernel, seq=seq, tm=tm),
        grid=(n_tok // tm,),
        in_specs=[pl.BlockSpec((SUBLANES, D_MODEL), lambda i: (jnp.maximum(i * halo - 1, 0), 0)),
                  pl.BlockSpec((tm, D_MODEL), lambda i: (i, 0)),
                  pl.BlockSpec((SUBLANES, D_MODEL), lambda i: (jnp.minimum((i + 1) * halo, last), 0)),
                  _mod_spec(layer, lambda i: mod_off + mod_stride * ((i * tm) // seq)),
                  half(0),
                  half(1),
                  _resident((D_FF, D_MODEL), layer),
                  _resident((3, SUBLANES, 2 * D_FF), layer),
                  _resident((SUBLANES, 2 * D_FF), layer),
                  _resident((2, D_MODEL), layer)],
        out_specs=pl.BlockSpec((tm, D_MODEL), lambda i: (i, 0)),
        out_shape=jax.ShapeDtypeStruct((n_tok, D_MODEL), F32),
        scratch_shapes=[pltpu.VMEM((D_MODEL // LANES, n_virt, LANES), F32),
                        pltpu.VMEM((n_virt, D_MODEL), BF16)]
        + [pltpu.VMEM((n_virt, FFN_CHUNK), F32)] * 4
        + [pltpu.VMEM((n_virt, D_FF), BF16), pltpu.VMEM((n_virt, D_MODEL), F32),
           pltpu.VMEM((D_MODEL // LANES, n_virt, LANES), F32)],
        compiler_params=_params(1),
        name="ffn_latent" if mod_stride else "ffn_context",
    )(x, x, x, mod, w_up, w_up, wd, fw, fb, lvec)


def _dft_cos_sin(n):
    idx = np.arange(n, dtype=np.int64)
    ang = 2.0 * np.pi * ((idx[:, None] * idx[None, :]) % n) / n
    return np.cos(ang) / np.sqrt(n), np.sin(ang) / np.sqrt(n)


def _channel_dft_table():
    c, s = _dft_cos_sin(FNET_GC)
    out = np.zeros((FNET_CH, 2 * FNET_CH), np.float32)
    for gi in range(FNET_GROUPS):
        sl = slice(gi * FNET_GC, (gi + 1) * FNET_GC)
        out[sl, sl] = c
        out[sl, FNET_CH + gi * FNET_GC:FNET_CH + (gi + 1) * FNET_GC] = s
    return jnp.asarray(out, dtype=BF16)


def _position_dft_table(n):
    c, s = _dft_cos_sin(n)
    return jnp.asarray(np.concatenate([c, -s], axis=1).astype(np.float32), dtype=BF16)


def _rope_tables(length):
    n_rows = length // GRID_W
    row = jnp.repeat(jnp.arange(n_rows, dtype=F32), GRID_W)
    col = jnp.tile(jnp.arange(GRID_W, dtype=F32), n_rows)
    quarter = HEAD_DIM // 4
    inv = ROPE_THETA ** (-jnp.arange(quarter, dtype=F32) / quarter)
    ang = jnp.concatenate([row[:, None] * inv, col[:, None] * inv], axis=-1)
    cos, sin = jnp.cos(ang), jnp.sin(ang)
    reps = LANES // HEAD_DIM
    return (jnp.tile(jnp.concatenate([cos, cos], axis=-1), (1, reps)),
            jnp.tile(jnp.concatenate([-sin, sin], axis=-1), (1, reps)))


def kernel(x_prompt, x_sample, cache_k, cache_v, c, c_ctx, w_mod, b_mod, w_in, b_in, sink,
           w_attn_o, conv_w, conv_b, conv_ln_g, conv_ln_b, w_conv_o, w_fnet, b_fnet, w_o,
           ln1_g, ln1_b, w_up, ffn_conv_w, ffn_conv_b, w_down, ln2_g, ln2_b):
    batch, seq_p, _ = x_prompt.shape
    dec_batch, seq_s, _ = x_sample.shape
    past = cache_k.shape[2]
    assert 1 + dec_batch <= MOD_ROWS

    cc = jnp.concatenate([c_ctx[None, :], c], axis=0)
    cc = jnp.pad(cc, ((0, MOD_ROWS - cc.shape[0]), (0, 0)))
    mods = _modulation(cc, w_mod, b_mod).reshape(DEPTH, MOD_ROWS, 6, D_MODEL)

    bcs = _channel_dft_table()
    cs_p = _position_dft_table(seq_p)
    cs_s = _position_dft_table(seq_s)
    rope = _rope_tables(seq_s)
    ck = cache_k.reshape(dec_batch, DEPTH, past, KV_W)
    cv = cache_v.reshape(dec_batch, DEPTH, past, KV_W)

    w_in_b, wa, wc, wf, wo, w_up_b, wd = (
        w.astype(BF16) for w in (w_in, w_attn_o, w_conv_o, w_fnet, w_o, w_up, w_down))
    b_in3 = b_in[:, None, :]
    cw = jnp.broadcast_to(conv_w[:, :, None, :], (DEPTH, CONV_K, SUBLANES, CONV_CH))
    cvec = jnp.stack([conv_b, conv_ln_g, conv_ln_b], axis=1)
    dvec = jnp.stack([b_fnet, ln1_g, ln1_b], axis=1)
    fw = jnp.broadcast_to(ffn_conv_w[:, :, None, :], (DEPTH, 3, SUBLANES, 2 * D_FF))
    fb = jnp.broadcast_to(ffn_conv_b[:, None, :], (DEPTH, SUBLANES, 2 * D_FF))
    lvec = jnp.stack([ln2_g, ln2_b], axis=1)

    xp = x_prompt.reshape(batch * seq_p, D_MODEL)
    xs = x_sample.reshape(dec_batch * seq_s, D_MODEL)
    new_k, new_v = [], []
    for l in range(DEPTH):
        def run(x, mod_off, mod_stride, seq, rope_tabs, attend, cs):
            q, k, v, g, fcs, gates = _inproj(x, mods, l, mod_off, mod_stride, seq, w_in_b, b_in3, bcs,
                                             rope_tabs)
            attn = attend(q, k, v)
            x = _mix(x, mods, l, mod_off, mod_stride, seq, attn, g, fcs, cs, gates, wa, wc, wf, wo,
                     cw, cvec, dvec)
            x = _ffn(x, mods, l, mod_off, mod_stride, seq, w_up_b, wd, fw, fb, lvec)
            return x, k, v

        xp, kp, vp = run(xp, 0, 0, seq_p, None,
                         lambda q, k, v: _ctx_attn(q, k, v, sink[l], seq_p), cs_p)
        new_k.append(kp.reshape(batch, seq_p, N_KV_HEADS, HEAD_DIM))
        new_v.append(vp.reshape(batch, seq_p, N_KV_HEADS, HEAD_DIM))
        xs, _, _ = run(xs, 1, 1, seq_s, rope,
                       lambda q, k, v: _win_attn(q, k, v, ck, cv, l, sink[l], seq_s), cs_s)

    return (xp.reshape(batch, seq_p, D_MODEL), xs.reshape(dec_batch, seq_s, D_MODEL),
            jnp.stack(new_k, axis=1), jnp.stack(new_v, axis=1))
```

```python
import functools
import math

import jax
import jax.numpy as jnp
import numpy as np
from jax import lax
from jax.experimental import pallas as pl
from jax.experimental.pallas import tpu as pltpu

D_MODEL = 1024
DEPTH = 2
GRID_W = 64
N_HEADS = 8
N_KV_HEADS = 2
HEAD_DIM = 64
ATTN_W = N_HEADS * HEAD_DIM
KV_W = N_KV_HEADS * HEAD_DIM
WINDOW = 128
BLOCK = 128
CONV_CH = D_MODEL // 4
CONV_K = 31
CONV_PAD = 16
FNET_GROUPS = 4
FNET_CH = D_MODEL // 4
FNET_GC = FNET_CH // FNET_GROUPS
N_GATES = 3 * D_MODEL
N_IN = ATTN_W + 2 * KV_W + 2 * CONV_CH + FNET_CH + N_GATES
D_FF = int(math.ceil(8 * D_MODEL / 3 / 128)) * 128
FFN_CHUNK = 256
FFN_ROWS = 48
ROPE_THETA = 10000.0
ALPHA = (2 * DEPTH) ** 0.25
NEG_INF = -1e30
LN_EPS = 1e-6

OFF_K = ATTN_W
OFF_V = OFF_K + KV_W
OFF_CA = OFF_V + KV_W
OFF_CB = OFF_CA + CONV_CH
OFF_F = OFF_CB + CONV_CH
OFF_G = OFF_F + FNET_CH

LANES = 128
SUBLANES = 8
VMEM_LIMIT = 56 * 1024 * 1024
MOD_ROWS = 16
ROW_PIECE = 32

F32 = jnp.float32
BF16 = jnp.bfloat16


def _resident(shape, layer=None):
    nd = len(shape)
    if layer is None:
        return pl.BlockSpec(shape, lambda *_: (0,) * nd, pipeline_mode=pl.Buffered(1))
    return pl.BlockSpec((None,) + tuple(shape), lambda *_: (layer,) + (0,) * nd,
                        pipeline_mode=pl.Buffered(1))


def _mod_spec(layer, row_of):
    return pl.BlockSpec((None, 1, 6, D_MODEL), lambda *idx: (layer, row_of(*idx), 0, 0))


def _params(n_axes):
    return pltpu.CompilerParams(dimension_semantics=("arbitrary",) * n_axes,
                                vmem_limit_bytes=VMEM_LIMIT)


def _sigmoid(x):
    return 1.0 / (1.0 + jnp.exp(-x))


def _ln(x):
    mu = jnp.mean(x, axis=-1, keepdims=True)
    xc = x - mu
    var = jnp.mean(xc * xc, axis=-1, keepdims=True)
    return xc * lax.rsqrt(var + LN_EPS)


def _dot(a, b):
    return jnp.dot(a, b, preferred_element_type=F32)


def _mod_kernel(c_ref, w_ref, b_ref, o_ref):
    c = c_ref[...]
    a = (c * _sigmoid(c)).astype(BF16)
    o_ref[0] = _dot(a, w_ref[0].astype(BF16)) + b_ref[0]


def _modulation(cc, w_mod, b_mod):
    tn = 1536
    n_out = w_mod.shape[-1]
    return pl.pallas_call(
        _mod_kernel,
        grid=(DEPTH, n_out // tn),
        in_specs=[pl.BlockSpec((MOD_ROWS, D_MODEL), lambda l, j: (0, 0)),
                  pl.BlockSpec((1, D_MODEL, tn), lambda l, j: (l, 0, j)),
                  pl.BlockSpec((1, 1, tn), lambda l, j: (l, 0, j))],
        out_specs=pl.BlockSpec((1, MOD_ROWS, tn), lambda l, j: (l, 0, j)),
        out_shape=jax.ShapeDtypeStruct((DEPTH, MOD_ROWS, n_out), F32),
        compiler_params=_params(2),
        name="modulation",
    )(cc, w_mod, b_mod.reshape(DEPTH, 1, n_out))


def _rope_tile(t, cos, sin):
    lane = lax.broadcasted_iota(jnp.int32, t.shape, 1)
    first_half = (lane % HEAD_DIM) < (HEAD_DIM // 2)
    partner = jnp.where(first_half, pltpu.roll(t, LANES - HEAD_DIM // 2, 1),
                        pltpu.roll(t, HEAD_DIM // 2, 1))
    return t * cos + partner * sin


def _inproj_kernel(*refs, latent):
    if latent:
        (x_ref, mod_ref, w_ref, b_ref, bcs_ref, cos_ref, sin_ref,
         q_ref, k_ref, v_ref, g_ref, fcs_ref, gate_ref, hb_ref, z_ref) = refs
    else:
        (x_ref, mod_ref, w_ref, b_ref, bcs_ref,
         q_ref, k_ref, v_ref, g_ref, fcs_ref, gate_ref, hb_ref, z_ref) = refs
    tm = x_ref.shape[0]
    scale = 1.0 + mod_ref[0, 1:2, :]
    shift = mod_ref[0, 0:1, :]
    for rp in range(tm // ROW_PIECE):
        rows = slice(rp * ROW_PIECE, (rp + 1) * ROW_PIECE)
        hb_ref[rows, :] = (_ln(x_ref[rows, :]) * scale + shift).astype(BF16)

    z_ref[...] = _dot(hb_ref[...], w_ref[:, 0:OFF_G]) + b_ref[:, 0:OFF_G]
    for t in range(3):
        c0 = OFF_G + t * D_MODEL
        gate_ref[:, t * D_MODEL:(t + 1) * D_MODEL] = _sigmoid(
            _dot(hb_ref[...], w_ref[:, c0:c0 + D_MODEL]) + b_ref[:, c0:c0 + D_MODEL]).astype(BF16)

    head_scale = HEAD_DIM ** -0.5
    for t in range(ATTN_W // LANES):
        qt = z_ref[:, t * LANES:(t + 1) * LANES]
        if latent:
            qt = _rope_tile(qt, cos_ref[...], sin_ref[...])
        q_ref[:, t * LANES:(t + 1) * LANES] = (qt * head_scale).astype(BF16)
    kt = z_ref[:, OFF_K:OFF_V]
    if latent:
        kt = _rope_tile(kt, cos_ref[...], sin_ref[...])
    k_ref[...] = kt
    v_ref[...] = z_ref[:, OFF_V:OFF_CA]
    g_ref[...] = z_ref[:, OFF_CA:OFF_CB] * _sigmoid(z_ref[:, OFF_CB:OFF_F])
    fcs_ref[...] = _dot(z_ref[:, OFF_F:OFF_G].astype(BF16), bcs_ref[...]).astype(BF16)


def _inproj(x, mod, layer, mod_off, mod_stride, seq, w_in, b_in, bcs, rope):
    tm = min(512, seq)
    n_tok = x.shape[0]
    latent = rope is not None
    tok = lambda i: (i, 0)
    in_specs = [pl.BlockSpec((tm, D_MODEL), tok),
                _mod_spec(layer, lambda i: mod_off + mod_stride * ((i * tm) // seq)),
                _resident((D_MODEL, N_IN), layer),
                _resident((1, N_IN), layer),
                _resident((FNET_CH, 2 * FNET_CH))]
    args = [x, mod, w_in, b_in, bcs]
    if latent:
        pos = lambda i: (i % (seq // tm), 0)
        in_specs += [pl.BlockSpec((tm, LANES), pos), pl.BlockSpec((tm, LANES), pos)]
        args += list(rope)
    widths = (ATTN_W, KV_W, KV_W, CONV_CH, 2 * FNET_CH, N_GATES)
    dtypes = (BF16, F32, F32, F32, BF16, BF16)
    return pl.pallas_call(
        functools.partial(_inproj_kernel, latent=latent),
        grid=(n_tok // tm,),
        in_specs=in_specs,
        out_specs=[pl.BlockSpec((tm, w), tok) for w in widths],
        out_shape=[jax.ShapeDtypeStruct((n_tok, w), d) for w, d in zip(widths, dtypes)],
        scratch_shapes=[pltpu.VMEM((tm, D_MODEL), BF16), pltpu.VMEM((tm, OFF_G), F32)],
        compiler_params=_params(1),
        name="inproj_latent" if latent else "inproj_context",
    )(*args)


def _pair_halves(x, kv):
    lane = lax.broadcasted_iota(jnp.int32, x.shape, 1)
    low = lane < HEAD_DIM
    swapped = pltpu.roll(x, HEAD_DIM, 1)
    lo_src, hi_src = (x, swapped) if kv == 0 else (swapped, x)
    return jnp.where(low, lo_src, 0.0).astype(BF16), jnp.where(low, 0.0, hi_src).astype(BF16)


def _softmax_terms(s, sink):
    m = jnp.maximum(jnp.max(s, axis=-1, keepdims=True), sink)
    p = jnp.exp(s - m)
    denom = jnp.sum(p, axis=-1, keepdims=True) + jnp.exp(sink - m)
    return p, denom


def _attend_group(q_ref, o_ref, sink_ref, kv, kb, vb, mask):
    tq = q_ref.shape[0]
    n = kb.shape[0] // 2
    t0 = kv * (N_HEADS // N_KV_HEADS) // 2
    q2 = jnp.concatenate([q_ref[:, t0 * LANES:(t0 + 1) * LANES],
                          q_ref[:, (t0 + 1) * LANES:(t0 + 2) * LANES]], axis=0)
    s = lax.dot_general(q2, kb, (((1,), (1,)), ((), ())), preferred_element_type=F32)
    first_tile = lax.broadcasted_iota(jnp.int32, (2 * tq, 1), 0) < tq
    parts, inv = [], []
    for half in range(2):
        sh = s[:, half * n:(half + 1) * n]
        if mask is not None:
            nm = mask.shape[1]
            sh = jnp.concatenate([jnp.where(mask, sh[:, :nm], NEG_INF), sh[:, nm:]], axis=1)
        head = 2 * t0 + half
        sink = jnp.where(first_tile, sink_ref[head], sink_ref[head + 2])
        p, denom = _softmax_terms(sh, sink)
        parts.append(p.astype(BF16))
        inv.append(1.0 / denom)
    o = _dot(jnp.concatenate(parts, axis=1), vb)
    lane = lax.broadcasted_iota(jnp.int32, o.shape, 1)
    o = (o * jnp.where(lane < HEAD_DIM, inv[0], inv[1])).astype(o_ref.dtype)
    o_ref[:, t0 * LANES:(t0 + 1) * LANES] = o[:tq]
    o_ref[:, (t0 + 1) * LANES:(t0 + 2) * LANES] = o[tq:]


def _ctx_attn_kernel(sink_ref, q_ref, k_ref, v_ref, o_ref):
    for kv in range(N_KV_HEADS):
        kb = jnp.concatenate(_pair_halves(k_ref[...], kv), axis=0)
        vb = jnp.concatenate(_pair_halves(v_ref[...], kv), axis=0)
        _attend_group(q_ref, o_ref, sink_ref, kv, kb, vb, None)


def _ctx_attn(q, k, v, sink, seq):
    n_tok = q.shape[0]
    tok = lambda b: (b, 0)
    return pl.pallas_call(
        _ctx_attn_kernel,
        grid=(n_tok // seq,),
        in_specs=[pl.BlockSpec(memory_space=pltpu.SMEM),
                  pl.BlockSpec((seq, ATTN_W), tok),
                  pl.BlockSpec((seq, KV_W), tok),
                  pl.BlockSpec((seq, KV_W), tok)],
        out_specs=pl.BlockSpec((seq, ATTN_W), tok),
        out_shape=jax.ShapeDtypeStruct((n_tok, ATTN_W), BF16),
        compiler_params=_params(1),
        name="context_attention",
    )(sink, q, k, v)


def _win_attn_kernel(sink_ref, q_ref, k_ref, v_ref, ck_ref, cv_ref, o_ref,
                     kl_ref, vl_ref, kc_ref, vc_ref, *, seq):
    qb = pl.program_id(1)

    @pl.when(qb == 0)
    def _():
        for kv in range(N_KV_HEADS):
            for src, dst in ((k_ref[...], kl_ref), (v_ref[...], vl_ref),
                             (ck_ref[0, 0], kc_ref), (cv_ref[0, 0], vc_ref)):
                lo, hi = _pair_halves(src, kv)
                dst[kv, 0] = lo
                dst[kv, 1] = hi

    span = 3 * BLOCK
    start = pl.multiple_of(jnp.clip(qb * BLOCK - BLOCK, 0, seq - span), BLOCK)
    qpos = qb * BLOCK + lax.broadcasted_iota(jnp.int32, (BLOCK, span), 0)
    kpos = start + lax.broadcasted_iota(jnp.int32, (BLOCK, span), 1)
    mask = jnp.abs(qpos - kpos) <= WINDOW
    mask = jnp.concatenate([mask, mask], axis=0)
    for kv in range(N_KV_HEADS):
        kb = jnp.concatenate([kl_ref[kv, 0, pl.ds(start, span), :], kc_ref[kv, 0],
                              kl_ref[kv, 1, pl.ds(start, span), :], kc_ref[kv, 1]], axis=0)
        vb = jnp.concatenate([vl_ref[kv, 0, pl.ds(start, span), :], vc_ref[kv, 0],
                              vl_ref[kv, 1, pl.ds(start, span), :], vc_ref[kv, 1]], axis=0)
        _attend_group(q_ref, o_ref, sink_ref, kv, kb, vb, mask)


def _win_attn(q, k, v, cache_k, cache_v, layer, sink, seq):
    n_tok = q.shape[0]
    bsz = n_tok // seq
    nb = seq // BLOCK
    past = cache_k.shape[2]
    qmap = lambda b, i: (b * nb + i, 0)
    kvmap = lambda b, i: (b, 0)
    cmap = lambda b, i: (b, layer, 0, 0)
    return pl.pallas_call(
        functools.partial(_win_attn_kernel, seq=seq),
        grid=(bsz, nb),
        in_specs=[pl.BlockSpec(memory_space=pltpu.SMEM),
                  pl.BlockSpec((BLOCK, ATTN_W), qmap),
                  pl.BlockSpec((seq, KV_W), kvmap),
                  pl.BlockSpec((seq, KV_W), kvmap),
                  pl.BlockSpec((1, 1, past, KV_W), cmap),
                  pl.BlockSpec((1, 1, past, KV_W), cmap)],
        out_specs=pl.BlockSpec((BLOCK, ATTN_W), qmap),
        out_shape=jax.ShapeDtypeStruct((n_tok, ATTN_W), BF16),
        scratch_shapes=[pltpu.VMEM((N_KV_HEADS, 2, seq, KV_W), BF16)] * 2
        + [pltpu.VMEM((N_KV_HEADS, 2, past, KV_W), BF16)] * 2,
        compiler_params=_params(2),
        name="window_attention",
    )(sink, q, k, v, cache_k, cache_v)


CONV_ROWS = 32


def _mix_kernel(x_ref, mod_ref, attn_ref, g_ref, fcs_ref, cs_ref, gate_ref,
                wa_ref, wc_ref, wf_ref, wo_ref, cw_ref, cvec_ref, dvec_ref,
                o_ref, gp_ref, gs_ref, u_ref, m_ref, mb_ref, *, seq, tm):
    i = pl.program_id(1)
    nblk = seq // tm
    r0 = pl.multiple_of(i * tm, tm)

    gp_ref[CONV_PAD:CONV_PAD + tm, :] = g_ref[pl.ds(r0, tm), :]
    top = g_ref[pl.ds(pl.multiple_of(jnp.maximum(r0 - CONV_PAD, 0), SUBLANES), CONV_PAD), :]
    gp_ref[0:CONV_PAD, :] = jnp.where(i > 0, top, 0.0)
    bot = g_ref[pl.ds(pl.multiple_of(jnp.minimum(r0 + tm, seq - CONV_PAD), SUBLANES), CONV_PAD), :]
    gp_ref[CONV_PAD + tm:, :] = jnp.where(i < nblk - 1, bot, 0.0)
    n_pad = tm + 2 * CONV_PAD
    staged = gp_ref[...]
    for r in range(1, SUBLANES):
        gs_ref[r - 1] = pltpu.roll(staged, n_pad - r, 0)
    first = CONV_PAD - CONV_K // 2
    n_tile = CONV_ROWS // SUBLANES
    for rc in range(tm // CONV_ROWS):
        accs = [None] * n_tile
        for j in range(CONV_K):
            a, r = divmod(first + j, SUBLANES)
            src = gp_ref if r == 0 else gs_ref.at[r - 1]
            w8 = cw_ref[j]
            for t in range(n_tile):
                lo = rc * CONV_ROWS + (a + t) * SUBLANES
                term = w8 * src[lo:lo + SUBLANES, :]
                accs[t] = term if accs[t] is None else accs[t] + term
        acc = jnp.concatenate(accs, axis=0)
        y = _ln(acc + cvec_ref[0:1, :]) * cvec_ref[1:2, :] + cvec_ref[2:3, :]
        u_ref[rc * CONV_ROWS:(rc + 1) * CONV_ROWS, :] = (y * _sigmoid(y)).astype(BF16)

    fm = (_dot(cs_ref[pl.ds(r0, tm), 0:seq], fcs_ref[:, 0:FNET_CH])
          + _dot(cs_ref[pl.ds(r0, tm), seq:], fcs_ref[:, FNET_CH:]))

    m_ref[...] = gate_ref[:, 0:D_MODEL].astype(F32) * _dot(attn_ref[...], wa_ref[...])
    m_ref[...] += gate_ref[:, D_MODEL:2 * D_MODEL].astype(F32) * _dot(u_ref[...], wc_ref[...])
    mb_ref[...] = (m_ref[...] + gate_ref[:, 2 * D_MODEL:].astype(F32)
                   * (_dot(fm.astype(BF16), wf_ref[...]) + dvec_ref[0:1, :])).astype(BF16)
    m_ref[...] = _dot(mb_ref[...], wo_ref[...])
    gain = mod_ref[0, 2:3, :]
    for rp in range(tm // ROW_PIECE):
        rows = slice(rp * ROW_PIECE, (rp + 1) * ROW_PIECE)
        r = ALPHA * x_ref[rows, :] + gain * m_ref[rows, :]
        o_ref[rows, :] = _ln(r) * dvec_ref[1:2, :] + dvec_ref[2:3, :]


def _mix(x, mod, layer, mod_off, mod_stride, seq, attn, g, fcs, cs, gates, wa, wc, wf, wo, cw, cvec,
         dvec):
    n_tok = x.shape[0]
    tm = min(512, seq)
    bsz = n_tok // seq
    nblk = seq // tm
    tok = lambda b, i: (b * nblk + i, 0)
    per_seq = lambda b, i: (b, 0)
    return pl.pallas_call(
        functools.partial(_mix_kernel, seq=seq, tm=tm),
        grid=(bsz, nblk),
        in_specs=[pl.BlockSpec((tm, D_MODEL), tok),
                  _mod_spec(layer, lambda b, i: mod_off + mod_stride * b),
                  pl.BlockSpec((tm, ATTN_W), tok),
                  pl.BlockSpec((seq, CONV_CH), per_seq),
                  pl.BlockSpec((seq, 2 * FNET_CH), per_seq),
                  _resident((seq, 2 * seq)),
                  pl.BlockSpec((tm, N_GATES), tok),
                  _resident((ATTN_W, D_MODEL), layer),
                  _resident((CONV_CH, D_MODEL), layer),
                  _resident((FNET_CH, D_MODEL), layer),
                  _resident((D_MODEL, D_MODEL), layer),
                  _resident((CONV_K, SUBLANES, CONV_CH), layer),
                  _resident((3, CONV_CH), layer),
                  _resident((3, D_MODEL), layer)],
        out_specs=pl.BlockSpec((tm, D_MODEL), tok),
        out_shape=jax.ShapeDtypeStruct((n_tok, D_MODEL), F32),
        scratch_shapes=[pltpu.VMEM((tm + 2 * CONV_PAD, CONV_CH), F32),
                        pltpu.VMEM((SUBLANES - 1, tm + 2 * CONV_PAD, CONV_CH), F32),
                        pltpu.VMEM((tm, CONV_CH), BF16),
                        pltpu.VMEM((tm, D_MODEL), F32),
                        pltpu.VMEM((tm, D_MODEL), BF16)],
        compiler_params=_params(2),
        name="mix_latent" if mod_stride else "mix_context",
    )(x, mod, attn, g, fcs, cs, gates, wa, wc, wf, wo, cw, cvec, dvec)


def _ffn_kernel(xp_ref, x_ref, xn_ref, mod_ref, wug_ref, wuv_ref, wd_ref, fw_ref, fb_ref, lvec_ref,
                o_ref, xs_ref, ext_ref, ug0_ref, uv0_ref, ug1_ref, uv1_ref, act_ref, y_ref, ys_ref,
                *, seq, tm):
    i = pl.program_id(0)
    r0 = i * tm
    n_slab = D_MODEL // LANES
    pitch = tm // SUBLANES + 2
    n_virt = SUBLANES * pitch
    scale = 1.0 + mod_ref[0, 4:5, :]
    shift = mod_ref[0, 3:4, :]
    gain = mod_ref[0, 5:6, :]
    n_chunk = D_FF // FFN_CHUNK
    row8 = lax.broadcasted_iota(jnp.int32, (SUBLANES, D_MODEL), 0)
    pieces = [(lo, min(lo + FFN_ROWS, n_virt)) for lo in range(0, n_virt, FFN_ROWS)]

    n_seg = max(1, tm // seq)
    if n_seg == 1:
        segments = [(0, 0, tm)]
    else:
        assert n_seg * (seq + SUBLANES) == n_virt
        segments = [(k * (seq + SUBLANES), k * seq, seq) for k in range(n_seg)]

    tail = jnp.concatenate([jnp.where(row8 == 0, xn_ref[...], 0.0),
                            jnp.where(row8 == SUBLANES - 1, xp_ref[...], 0.0)], axis=0)
    for c in range(n_slab):
        lanes = slice(c * LANES, (c + 1) * LANES)
        for virt, real, length in segments:
            xs_ref[c, virt:virt + length, :] = x_ref[real:real + length, lanes]
        if n_seg == 1:
            xs_ref[c, tm:n_virt, :] = tail[:, lanes]
        else:
            for virt, _, length in segments:
                xs_ref[c, virt + length:virt + length + SUBLANES, :] = jnp.zeros((SUBLANES, LANES), F32)

    def strided_rows(j):
        return jnp.concatenate(
            [xs_ref[c, pl.ds(j, SUBLANES, stride=pitch), :] for c in range(n_slab)], axis=1)

    no_next = (r0 + tm) % seq == 0
    no_prev = r0 % seq == 0
    k16 = lax.broadcasted_iota(jnp.int32, (2 * SUBLANES, 1), 0)
    for jj in range(0, pitch, 2):
        xj = jnp.concatenate([strided_rows(jj), strided_rows(jj + 1)], axis=0)
        virt_row = jj + k16 // SUBLANES + pitch * (k16 % SUBLANES)
        if n_seg == 1:
            padding = ((virt_row == tm) & no_next) | ((virt_row == n_virt - 1) & no_prev)
        else:
            padding = virt_row % (seq + SUBLANES) >= seq
        h = jnp.where(padding, 0.0, _ln(xj) * scale + shift)
        ext_ref[jj * SUBLANES:(jj + 2) * SUBLANES, :] = h.astype(BF16)

    u_refs = ((ug0_ref, uv0_ref), (ug1_ref, uv1_ref))

    def up(c):
        g_ref, v_ref = u_refs[c % 2]
        lo = c * FFN_CHUNK
        g_ref[...] = _dot(ext_ref[...], wug_ref[:, lo:lo + FFN_CHUNK])
        v_ref[...] = _dot(ext_ref[...], wuv_ref[:, lo:lo + FFN_CHUNK])

    def conv3(u_ref, lo, hi, col):
        c0 = col * FFN_CHUNK
        if lo >= SUBLANES:
            prev = u_ref[lo - SUBLANES:hi - SUBLANES, :]
        else:
            prev = jnp.concatenate([pltpu.roll(u_ref[n_virt - SUBLANES:n_virt, :], 1, 0),
                                    u_ref[0:hi - SUBLANES, :]], axis=0)
        if hi + SUBLANES <= n_virt:
            nxt = u_ref[lo + SUBLANES:hi + SUBLANES, :]
        else:
            nxt = jnp.concatenate([u_ref[lo + SUBLANES:n_virt, :],
                                   pltpu.roll(u_ref[0:SUBLANES, :], SUBLANES - 1, 0)], axis=0)

        def rows(w8):
            return jnp.concatenate([w8] * ((hi - lo) // SUBLANES), axis=0)

        cols = slice(c0, c0 + FFN_CHUNK)
        return (rows(fw_ref[0, :, cols]) * prev + rows(fw_ref[1, :, cols]) * u_ref[lo:hi, :]
                + rows(fw_ref[2, :, cols]) * nxt + rows(fb_ref[:, cols]))

    def activate(c):
        g_ref, v_ref = u_refs[c % 2]
        for lo, hi in pieces:
            yg = conv3(g_ref, lo, hi, c)
            yv = conv3(v_ref, lo, hi, c + n_chunk)
            act_ref[lo:hi, c * FFN_CHUNK:(c + 1) * FFN_CHUNK] = (yg * _sigmoid(yg) * yv).astype(BF16)

    up(0)
    for c in range(n_chunk):
        if c + 1 < n_chunk:
            up(c + 1)
        activate(c)
    y_ref[...] = _dot(act_ref[...], wd_ref[...])

    for jj in range(0, pitch, 2):
        xj = jnp.concatenate([strided_rows(jj), strided_rows(jj + 1)], axis=0)
        r = ALPHA * xj + gain * y_ref[jj * SUBLANES:(jj + 2) * SUBLANES, :]
        out = _ln(r) * lvec_ref[0:1, :] + lvec_ref[1:2, :]
        for k in range(2):
            for c in range(n_slab):
                ys_ref[c, pl.ds(jj + k, SUBLANES, stride=pitch), :] = (
                    out[k * SUBLANES:(k + 1) * SUBLANES, c * LANES:(c + 1) * LANES])
    for c in range(n_slab):
        for virt, real, length in segments:
            o_ref[real:real + length, c * LANES:(c + 1) * LANES] = ys_ref[c, virt:virt + length, :]


def _ffn(x, mod, layer, mod_off, mod_stride, seq, w_up, wd, fw, fb, lvec):
    n_tok = x.shape[0]
    tm = 512
    assert seq % tm == 0 or tm % seq == 0
    halo = tm // SUBLANES
    last = n_tok // SUBLANES - 1
    n_virt = tm + 2 * SUBLANES
    half = lambda j: pl.BlockSpec((None, D_MODEL, D_FF), lambda i: (layer, 0, j),
                                  pipeline_mode=pl.Buffered(1))
    return pl.pallas_call(
        functools.partial(_ffn_kernel, seq=seq, tm=tm),
        grid=(n_tok // tm,),
        in_specs=[pl.BlockSpec((SUBLANES, D_MODEL), lambda i: (jnp.maximum(i * halo - 1, 0), 0)),
                  pl.BlockSpec((tm, D_MODEL), lambda i: (i, 0)),
                  pl.BlockSpec((SUBLANES, D_MODEL), lambda i: (jnp.minimum((i + 1) * halo, last), 0)),
                  _mod_spec(layer, lambda i: mod_off + mod_stride * ((i * tm) // seq)),
                  half(0),
                  half(1),
                  _resident((D_FF, D_MODEL), layer),
                  _resident((3, SUBLANES, 2 * D_FF), layer),
                  _resident((SUBLANES, 2 * D_FF), layer),
                  _resident((2, D_MODEL), layer)],
        out_specs=pl.BlockSpec((tm, D_MODEL), lambda i: (i, 0)),
        out_shape=jax.ShapeDtypeStruct((n_tok, D_MODEL), F32),
        scratch_shapes=[pltpu.VMEM((D_MODEL // LANES, n_virt, LANES), F32),
                        pltpu.VMEM((n_virt, D_MODEL), BF16)]
        + [pltpu.VMEM((n_virt, FFN_CHUNK), F32)] * 4
        + [pltpu.VMEM((n_virt, D_FF), BF16), pltpu.VMEM((n_virt, D_MODEL), F32),
           pltpu.VMEM((D_MODEL // LANES, n_virt, LANES), F32)],
        compiler_params=_params(1),
        name="ffn_latent" if mod_stride else "ffn_context",
    )(x, x, x, mod, w_up, w_up, wd, fw, fb, lvec)


def _dft_cos_sin(n):
    idx = np.arange(n, dtype=np.int64)
    ang = 2.0 * np.pi * ((idx[:, None] * idx[None, :]) % n) / n
    return np.cos(ang) / np.sqrt(n), np.sin(ang) / np.sqrt(n)


def _channel_dft_table():
    c, s = _dft_cos_sin(FNET_GC)
    out = np.zeros((FNET_CH, 2 * FNET_CH), np.float32)
    for gi in range(FNET_GROUPS):
        sl = slice(gi * FNET_GC, (gi + 1) * FNET_GC)
        out[sl, sl] = c
        out[sl, FNET_CH + gi * FNET_GC:FNET_CH + (gi + 1) * FNET_GC] = s
    return jnp.asarray(out, dtype=BF16)


def _position_dft_table(n):
    c, s = _dft_cos_sin(n)
    return jnp.asarray(np.concatenate([c, -s], axis=1).astype(np.float32), dtype=BF16)


def _rope_tables(length):
    n_rows = length // GRID_W
    row = jnp.repeat(jnp.arange(n_rows, dtype=F32), GRID_W)
    col = jnp.tile(jnp.arange(GRID_W, dtype=F32), n_rows)
    quarter = HEAD_DIM // 4
    inv = ROPE_THETA ** (-jnp.arange(quarter, dtype=F32) / quarter)
    ang = jnp.concatenate([row[:, None] * inv, col[:, None] * inv], axis=-1)
    cos, sin = jnp.cos(ang), jnp.sin(ang)
    reps = LANES // HEAD_DIM
    return (jnp.tile(jnp.concatenate([cos, cos], axis=-1), (1, reps)),
            jnp.tile(jnp.concatenate([-sin, sin], axis=-1), (1, reps)))


def kernel(x_prompt, x_sample, cache_k, cache_v, c, c_ctx, w_mod, b_mod, w_in, b_in, sink,
           w_attn_o, conv_w, conv_b, conv_ln_g, conv_ln_b, w_conv_o, w_fnet, b_fnet, w_o,
           ln1_g, ln1_b, w_up, ffn_conv_w, ffn_conv_b, w_down, ln2_g, ln2_b):
    batch, seq_p, _ = x_prompt.shape
    dec_batch, seq_s, _ = x_sample.shape
    past = cache_k.shape[2]
    assert 1 + dec_batch <= MOD_ROWS

    cc = jnp.concatenate([c_ctx[None, :], c], axis=0)
    cc = jnp.pad(cc, ((0, MOD_ROWS - cc.shape[0]), (0, 0)))
    mods = _modulation(cc, w_mod, b_mod).reshape(DEPTH, MOD_ROWS, 6, D_MODEL)

    bcs = _channel_dft_table()
    cs_p = _position_dft_table(seq_p)
    cs_s = _position_dft_table(seq_s)
    rope = _rope_tables(seq_s)
    ck = cache_k.reshape(dec_batch, DEPTH, past, KV_W)
    cv = cache_v.reshape(dec_batch, DEPTH, past, KV_W)

    w_in_b, wa, wc, wf, wo, w_up_b, wd = (
        w.astype(BF16) for w in (w_in, w_attn_o, w_conv_o, w_fnet, w_o, w_up, w_down))
    b_in3 = b_in[:, None, :]
    cw = jnp.broadcast_to(conv_w[:, :, None, :], (DEPTH, CONV_K, SUBLANES, CONV_CH))
    cvec = jnp.stack([conv_b, conv_ln_g, conv_ln_b], axis=1)
    dvec = jnp.stack([b_fnet, ln1_g, ln1_b], axis=1)
    fw = jnp.broadcast_to(ffn_conv_w[:, :, None, :], (DEPTH, 3, SUBLANES, 2 * D_FF))
    fb = jnp.broadcast_to(ffn_conv_b[:, None, :], (DEPTH, SUBLANES, 2 * D_FF))
    lvec = jnp.stack([ln2_g, ln2_b], axis=1)

    xp = x_prompt.reshape(batch * seq_p, D_MODEL)
    xs = x_sample.reshape(dec_batch * seq_s, D_MODEL)
    new_k, new_v = [], []
    for l in range(DEPTH):
        def run(x, mod_off, mod_stride, seq, rope_tabs, attend, cs):
            q, k, v, g, fcs, gates = _inproj(x, mods, l, mod_off, mod_stride, seq, w_in_b, b_in3, bcs,
                                             rope_tabs)
            attn = attend(q, k, v)
            x = _mix(x, mods, l, mod_off, mod_stride, seq, attn, g, fcs, cs, gates, wa, wc, wf, wo,
                     cw, cvec, dvec)
            x = _ffn(x, mods, l, mod_off, mod_stride, seq, w_up_b, wd, fw, fb, lvec)
            return x, k, v

        xp, kp, vp = run(xp, 0, 0, seq_p, None,
                         lambda q, k, v: _ctx_attn(q, k, v, sink[l], seq_p), cs_p)
        new_k.append(kp.reshape(batch, seq_p, N_KV_HEADS, HEAD_DIM))
        new_v.append(vp.reshape(batch, seq_p, N_KV_HEADS, HEAD_DIM))
        xs, _, _ = run(xs, 1, 1, seq_s, rope,
                       lambda q, k, v: _win_attn(q, k, v, ck, cv, l, sink[l], seq_s), cs_s)

    return (xp.reshape(batch, seq_p, D_MODEL), xs.reshape(dec_batch, seq_s, D_MODEL),
            jnp.stack(new_k, axis=1), jnp.stack(new_v, axis=1))
```

```python
import functools
import math

import jax
import jax.numpy as jnp
import numpy as np
from jax import lax
from jax.experimental import pallas as pl
from jax.experimental.pallas import tpu as pltpu

D_MODEL = 1024
DEPTH = 2
GRID_W = 64
N_HEADS = 8
N_KV_HEADS = 2
HEAD_DIM = 64
ATTN_W = N_HEADS * HEAD_DIM
KV_W = N_KV_HEADS * HEAD_DIM
WINDOW = 128
BLOCK = 128
WIN_QBLOCKS = 4
CONV_CH = D_MODEL // 4
CONV_K = 31
CONV_PAD = 16
FNET_GROUPS = 4
FNET_CH = D_MODEL // 4
FNET_GC = FNET_CH // FNET_GROUPS
N_GATES = 3 * D_MODEL
N_IN = ATTN_W + 2 * KV_W + 2 * CONV_CH + FNET_CH + N_GATES
D_FF = int(math.ceil(8 * D_MODEL / 3 / 128)) * 128
FFN_CHUNK = 256
FFN_ROWS = 48
ROPE_THETA = 10000.0
ALPHA = (2 * DEPTH) ** 0.25
NEG_INF = -1e30
LN_EPS = 1e-6

OFF_K = ATTN_W
OFF_V = OFF_K + KV_W
OFF_CA = OFF_V + KV_W
OFF_CB = OFF_CA + CONV_CH
OFF_F = OFF_CB + CONV_CH
OFF_G = OFF_F + FNET_CH

LANES = 128
SUBLANES = 8
VMEM_LIMIT = 56 * 1024 * 1024
MOD_ROWS = 16
ROW_PIECE = 32

F32 = jnp.float32
BF16 = jnp.bfloat16


def _resident(shape, layer=None):
    nd = len(shape)
    if layer is None:
        return pl.BlockSpec(shape, lambda *_: (0,) * nd, pipeline_mode=pl.Buffered(1))
    return pl.BlockSpec((None,) + tuple(shape), lambda *_: (layer,) + (0,) * nd,
                        pipeline_mode=pl.Buffered(1))


def _mod_spec(layer, row_of):
    return pl.BlockSpec((None, 1, 6, D_MODEL), lambda *idx: (layer, row_of(*idx), 0, 0))


def _params(n_axes):
    return pltpu.CompilerParams(dimension_semantics=("arbitrary",) * n_axes,
                                vmem_limit_bytes=VMEM_LIMIT)


def _sigmoid(x):
    return 1.0 / (1.0 + jnp.exp(-x))


def _ln(x):
    mu = jnp.mean(x, axis=-1, keepdims=True)
    xc = x - mu
    var = jnp.mean(xc * xc, axis=-1, keepdims=True)
    return xc * lax.rsqrt(var + LN_EPS)


def _dot(a, b):
    return jnp.dot(a, b, preferred_element_type=F32)


def _mod_kernel(c_ref, w_ref, b_ref, o_ref):
    c = c_ref[...]
    a = (c * _sigmoid(c)).astype(BF16)
    o_ref[0] = _dot(a, w_ref[0].astype(BF16)) + b_ref[0]


def _modulation(cc, w_mod, b_mod):
    tn = 1536
    n_out = w_mod.shape[-1]
    return pl.pallas_call(
        _mod_kernel,
        grid=(DEPTH, n_out // tn),
        in_specs=[pl.BlockSpec((MOD_ROWS, D_MODEL), lambda l, j: (0, 0)),
                  pl.BlockSpec((1, D_MODEL, tn), lambda l, j: (l, 0, j)),
                  pl.BlockSpec((1, 1, tn), lambda l, j: (l, 0, j))],
        out_specs=pl.BlockSpec((1, MOD_ROWS, tn), lambda l, j: (l, 0, j)),
        out_shape=jax.ShapeDtypeStruct((DEPTH, MOD_ROWS, n_out), F32),
        compiler_params=_params(2),
        name="modulation",
    )(cc, w_mod, b_mod.reshape(DEPTH, 1, n_out))


def _rope_tile(t, cos, sin):
    lane = lax.broadcasted_iota(jnp.int32, t.shape, 1)
    first_half = (lane % HEAD_DIM) < (HEAD_DIM // 2)
    partner = jnp.where(first_half, pltpu.roll(t, LANES - HEAD_DIM // 2, 1),
                        pltpu.roll(t, HEAD_DIM // 2, 1))
    return t * cos + partner * sin


def _inproj_kernel(*refs, latent):
    if latent:
        (x_ref, mod_ref, w_ref, b_ref, bcs_ref, cos_ref, sin_ref,
         q_ref, k_ref, v_ref, g_ref, fcs_ref, gate_ref, hb_ref, z_ref) = refs
    else:
        (x_ref, mod_ref, w_ref, b_ref, bcs_ref,
         q_ref, k_ref, v_ref, g_ref, fcs_ref, gate_ref, hb_ref, z_ref) = refs
    tm = x_ref.shape[0]
    scale = 1.0 + mod_ref[0, 1:2, :]
    shift = mod_ref[0, 0:1, :]
    for rp in range(tm // ROW_PIECE):
        rows = slice(rp * ROW_PIECE, (rp + 1) * ROW_PIECE)
        hb_ref[rows, :] = (_ln(x_ref[rows, :]) * scale + shift).astype(BF16)

    z_ref[...] = _dot(hb_ref[...], w_ref[:, 0:OFF_G]) + b_ref[:, 0:OFF_G]
    for t in range(3):
        c0 = OFF_G + t * D_MODEL
        gate_ref[:, t * D_MODEL:(t + 1) * D_MODEL] = _sigmoid(
            _dot(hb_ref[...], w_ref[:, c0:c0 + D_MODEL]) + b_ref[:, c0:c0 + D_MODEL]).astype(BF16)

    head_scale = HEAD_DIM ** -0.5
    for t in range(ATTN_W // LANES):
        qt = z_ref[:, t * LANES:(t + 1) * LANES]
        if latent:
            qt = _rope_tile(qt, cos_ref[...], sin_ref[...])
        q_ref[:, t * LANES:(t + 1) * LANES] = (qt * head_scale).astype(BF16)
    kt = z_ref[:, OFF_K:OFF_V]
    if latent:
        kt = _rope_tile(kt, cos_ref[...], sin_ref[...])
    k_ref[...] = kt
    v_ref[...] = z_ref[:, OFF_V:OFF_CA]
    g_ref[...] = z_ref[:, OFF_CA:OFF_CB] * _sigmoid(z_ref[:, OFF_CB:OFF_F])
    fcs_ref[...] = _dot(z_ref[:, OFF_F:OFF_G].astype(BF16), bcs_ref[...]).astype(BF16)


def _inproj(x, mod, layer, mod_off, mod_stride, seq, w_in, b_in, bcs, rope):
    tm = 256
    n_tok = x.shape[0]
    latent = rope is not None
    tok = lambda i: (i, 0)
    in_specs = [pl.BlockSpec((tm, D_MODEL), tok),
                _mod_spec(layer, lambda i: mod_off + mod_stride * ((i * tm) // seq)),
                _resident((D_MODEL, N_IN), layer),
                _resident((1, N_IN), layer),
                _resident((FNET_CH, 2 * FNET_CH))]
    args = [x, mod, w_in, b_in, bcs]
    if latent:
        pos = lambda i: (i % (seq // tm), 0)
        in_specs += [pl.BlockSpec((tm, LANES), pos), pl.BlockSpec((tm, LANES), pos)]
        args += list(rope)
    widths = (ATTN_W, KV_W, KV_W, CONV_CH, 2 * FNET_CH, N_GATES)
    dtypes = (BF16, F32, F32, F32, BF16, BF16)
    return pl.pallas_call(
        functools.partial(_inproj_kernel, latent=latent),
        grid=(n_tok // tm,),
        in_specs=in_specs,
        out_specs=[pl.BlockSpec((tm, w), tok) for w in widths],
        out_shape=[jax.ShapeDtypeStruct((n_tok, w), d) for w, d in zip(widths, dtypes)],
        scratch_shapes=[pltpu.VMEM((tm, D_MODEL), BF16), pltpu.VMEM((tm, OFF_G), F32)],
        compiler_params=_params(1),
        name="inproj_latent" if latent else "inproj_context",
    )(*args)


def _pair_halves(x, kv):
    lane = lax.broadcasted_iota(jnp.int32, x.shape, 1)
    low = lane < HEAD_DIM
    swapped = pltpu.roll(x, HEAD_DIM, 1)
    lo_src, hi_src = (x, swapped) if kv == 0 else (swapped, x)
    return jnp.where(low, lo_src, 0.0).astype(BF16), jnp.where(low, 0.0, hi_src).astype(BF16)


def _softmax_terms(s, sink):
    m = jnp.maximum(jnp.max(s, axis=-1, keepdims=True), sink)
    p = jnp.exp(s - m)
    denom = jnp.sum(p, axis=-1, keepdims=True) + jnp.exp(sink - m)
    return p, denom


def _attend_group(q_ref, o_ref, sink_ref, kv, kb, vb, mask, rows=slice(None)):
    n = kb.shape[0] // 2
    t0 = kv * (N_HEADS // N_KV_HEADS) // 2
    q2 = jnp.concatenate([q_ref[rows, t0 * LANES:(t0 + 1) * LANES],
                          q_ref[rows, (t0 + 1) * LANES:(t0 + 2) * LANES]], axis=0)
    tq = q2.shape[0] // 2
    s = lax.dot_general(q2, kb, (((1,), (1,)), ((), ())), preferred_element_type=F32)
    first_tile = lax.broadcasted_iota(jnp.int32, (2 * tq, 1), 0) < tq
    parts, inv = [], []
    for half in range(2):
        sh = s[:, half * n:(half + 1) * n]
        if mask is not None:
            nm = mask.shape[1]
            sh = jnp.concatenate([jnp.where(mask, sh[:, :nm], NEG_INF), sh[:, nm:]], axis=1)
        head = 2 * t0 + half
        sink = jnp.where(first_tile, sink_ref[head], sink_ref[head + 2])
        p, denom = _softmax_terms(sh, sink)
        parts.append(p.astype(BF16))
        inv.append(1.0 / denom)
    o = _dot(jnp.concatenate(parts, axis=1), vb)
    lane = lax.broadcasted_iota(jnp.int32, o.shape, 1)
    o = (o * jnp.where(lane < HEAD_DIM, inv[0], inv[1])).astype(o_ref.dtype)
    o_ref[rows, t0 * LANES:(t0 + 1) * LANES] = o[:tq]
    o_ref[rows, (t0 + 1) * LANES:(t0 + 2) * LANES] = o[tq:]


def _ctx_attn_kernel(sink_ref, q_ref, k_ref, v_ref, o_ref):
    for kv in range(N_KV_HEADS):
        kb = jnp.concatenate(_pair_halves(k_ref[...], kv), axis=0)
        vb = jnp.concatenate(_pair_halves(v_ref[...], kv), axis=0)
        _attend_group(q_ref, o_ref, sink_ref, kv, kb, vb, None)


def _ctx_attn(q, k, v, sink, seq):
    n_tok = q.shape[0]
    tok = lambda b: (b, 0)
    return pl.pallas_call(
        _ctx_attn_kernel,
        grid=(n_tok // seq,),
        in_specs=[pl.BlockSpec(memory_space=pltpu.SMEM),
                  pl.BlockSpec((seq, ATTN_W), tok),
                  pl.BlockSpec((seq, KV_W), tok),
                  pl.BlockSpec((seq, KV_W), tok)],
        out_specs=pl.BlockSpec((seq, ATTN_W), tok),
        out_shape=jax.ShapeDtypeStruct((n_tok, ATTN_W), BF16),
        compiler_params=_params(1),
        name="context_attention",
    )(sink, q, k, v)


def _win_attn_kernel(sink_ref, q_ref, k_ref, v_ref, ck_ref, cv_ref, o_ref,
                     kl_ref, vl_ref, kc_ref, vc_ref, *, seq):
    step = pl.program_id(1)

    @pl.when(step == 0)
    def _():
        for kv in range(N_KV_HEADS):
            for src, dst in ((k_ref[...], kl_ref), (v_ref[...], vl_ref),
                             (ck_ref[0, 0], kc_ref), (cv_ref[0, 0], vc_ref)):
                lo, hi = _pair_halves(src, kv)
                dst[kv, 0] = lo
                dst[kv, 1] = hi

    span = 3 * BLOCK
    for sb in range(WIN_QBLOCKS):
        qb = step * WIN_QBLOCKS + sb
        start = pl.multiple_of(jnp.clip(qb * BLOCK - BLOCK, 0, seq - span), BLOCK)
        qpos = qb * BLOCK + lax.broadcasted_iota(jnp.int32, (BLOCK, span), 0)
        kpos = start + lax.broadcasted_iota(jnp.int32, (BLOCK, span), 1)
        mask = jnp.abs(qpos - kpos) <= WINDOW
        mask = jnp.concatenate([mask, mask], axis=0)
        for kv in range(N_KV_HEADS):
            kb = jnp.concatenate([kl_ref[kv, 0, pl.ds(start, span), :], kc_ref[kv, 0],
                                  kl_ref[kv, 1, pl.ds(start, span), :], kc_ref[kv, 1]], axis=0)
            vb = jnp.concatenate([vl_ref[kv, 0, pl.ds(start, span), :], vc_ref[kv, 0],
                                  vl_ref[kv, 1, pl.ds(start, span), :], vc_ref[kv, 1]], axis=0)
            _attend_group(q_ref, o_ref, sink_ref, kv, kb, vb, mask,
                          rows=slice(sb * BLOCK, (sb + 1) * BLOCK))


def _win_attn(q, k, v, cache_k, cache_v, layer, sink, seq):
    n_tok = q.shape[0]
    bsz = n_tok // seq
    nb = seq // (BLOCK * WIN_QBLOCKS)
    past = cache_k.shape[2]
    qmap = lambda b, i: (b * nb + i, 0)
    kvmap = lambda b, i: (b, 0)
    cmap = lambda b, i: (b, layer, 0, 0)
    return pl.pallas_call(
        functools.partial(_win_attn_kernel, seq=seq),
        grid=(bsz, nb),
        in_specs=[pl.BlockSpec(memory_space=pltpu.SMEM),
                  pl.BlockSpec((WIN_QBLOCKS * BLOCK, ATTN_W), qmap),
                  pl.BlockSpec((seq, KV_W), kvmap),
                  pl.BlockSpec((seq, KV_W), kvmap),
                  pl.BlockSpec((1, 1, past, KV_W), cmap),
                  pl.BlockSpec((1, 1, past, KV_W), cmap)],
        out_specs=pl.BlockSpec((WIN_QBLOCKS * BLOCK, ATTN_W), qmap),
        out_shape=jax.ShapeDtypeStruct((n_tok, ATTN_W), BF16),
        scratch_shapes=[pltpu.VMEM((N_KV_HEADS, 2, seq, KV_W), BF16)] * 2
        + [pltpu.VMEM((N_KV_HEADS, 2, past, KV_W), BF16)] * 2,
        compiler_params=_params(2),
        name="window_attention",
    )(sink, q, k, v, cache_k, cache_v)


CONV_ROWS = 32


def _mix_kernel(x_ref, mod_ref, attn_ref, g_ref, fcs_ref, cs_ref, gate_ref,
                wa_ref, wc_ref, wf_ref, wo_ref, cw_ref, cvec_ref, dvec_ref,
                o_ref, gp_ref, gs_ref, u_ref, m_ref, mb_ref, *, seq, tm):
    i = pl.program_id(1)
    nblk = seq // tm
    r0 = pl.multiple_of(i * tm, tm)

    gp_ref[CONV_PAD:CONV_PAD + tm, :] = g_ref[pl.ds(r0, tm), :]
    top = g_ref[pl.ds(pl.multiple_of(jnp.maximum(r0 - CONV_PAD, 0), SUBLANES), CONV_PAD), :]
    gp_ref[0:CONV_PAD, :] = jnp.where(i > 0, top, 0.0)
    bot = g_ref[pl.ds(pl.multiple_of(jnp.minimum(r0 + tm, seq - CONV_PAD), SUBLANES), CONV_PAD), :]
    gp_ref[CONV_PAD + tm:, :] = jnp.where(i < nblk - 1, bot, 0.0)
    n_pad = tm + 2 * CONV_PAD
    staged = gp_ref[...]
    for r in range(1, SUBLANES):
        gs_ref[r - 1] = pltpu.roll(staged, n_pad - r, 0)
    first = CONV_PAD - CONV_K // 2
    n_tile = CONV_ROWS // SUBLANES
    for rc in range(tm // CONV_ROWS):
        accs = [None] * n_tile
        for j in range(CONV_K):
            a, r = divmod(first + j, SUBLANES)
            src = gp_ref if r == 0 else gs_ref.at[r - 1]
            w8 = cw_ref[j]
            for t in range(n_tile):
                lo = rc * CONV_ROWS + (a + t) * SUBLANES
                term = w8 * src[lo:lo + SUBLANES, :]
                accs[t] = term if accs[t] is None else accs[t] + term
        acc = jnp.concatenate(accs, axis=0)
        y = _ln(acc + cvec_ref[0:1, :]) * cvec_ref[1:2, :] + cvec_ref[2:3, :]
        u_ref[rc * CONV_ROWS:(rc + 1) * CONV_ROWS, :] = (y * _sigmoid(y)).astype(BF16)

    fm = (_dot(cs_ref[pl.ds(r0, tm), 0:seq], fcs_ref[:, 0:FNET_CH])
          + _dot(cs_ref[pl.ds(r0, tm), seq:], fcs_ref[:, FNET_CH:]))

    m_ref[...] = gate_ref[:, 0:D_MODEL].astype(F32) * _dot(attn_ref[...], wa_ref[...])
    m_ref[...] += gate_ref[:, D_MODEL:2 * D_MODEL].astype(F32) * _dot(u_ref[...], wc_ref[...])
    mb_ref[...] = (m_ref[...] + gate_ref[:, 2 * D_MODEL:].astype(F32)
                   * (_dot(fm.astype(BF16), wf_ref[...]) + dvec_ref[0:1, :])).astype(BF16)
    m_ref[...] = _dot(mb_ref[...], wo_ref[...])
    gain = mod_ref[0, 2:3, :]
    for rp in range(tm // ROW_PIECE):
        rows = slice(rp * ROW_PIECE, (rp + 1) * ROW_PIECE)
        r = ALPHA * x_ref[rows, :] + gain * m_ref[rows, :]
        o_ref[rows, :] = _ln(r) * dvec_ref[1:2, :] + dvec_ref[2:3, :]


def _mix(x, mod, layer, mod_off, mod_stride, seq, attn, g, fcs, cs, gates, wa, wc, wf, wo, cw, cvec,
         dvec):
    n_tok = x.shape[0]
    tm = min(512, seq)
    bsz = n_tok // seq
    nblk = seq // tm
    tok = lambda b, i: (b * nblk + i, 0)
    per_seq = lambda b, i: (b, 0)
    return pl.pallas_call(
        functools.partial(_mix_kernel, seq=seq, tm=tm),
        grid=(bsz, nblk),
        in_specs=[pl.BlockSpec((tm, D_MODEL), tok),
                  _mod_spec(layer, lambda b, i: mod_off + mod_stride * b),
                  pl.BlockSpec((tm, ATTN_W), tok),
                  pl.BlockSpec((seq, CONV_CH), per_seq),
                  pl.BlockSpec((seq, 2 * FNET_CH), per_seq),
                  _resident((seq, 2 * seq)),
                  pl.BlockSpec((tm, N_GATES), tok),
                  _resident((ATTN_W, D_MODEL), layer),
                  _resident((CONV_CH, D_MODEL), layer),
                  _resident((FNET_CH, D_MODEL), layer),
                  _resident((D_MODEL, D_MODEL), layer),
                  _resident((CONV_K, SUBLANES, CONV_CH), layer),
                  _resident((3, CONV_CH), layer),
                  _resident((3, D_MODEL), layer)],
        out_specs=pl.BlockSpec((tm, D_MODEL), tok),
        out_shape=jax.ShapeDtypeStruct((n_tok, D_MODEL), F32),
        scratch_shapes=[pltpu.VMEM((tm + 2 * CONV_PAD, CONV_CH), F32),
                        pltpu.VMEM((SUBLANES - 1, tm + 2 * CONV_PAD, CONV_CH), F32),
                        pltpu.VMEM((tm, CONV_CH), BF16),
                        pltpu.VMEM((tm, D_MODEL), F32),
                        pltpu.VMEM((tm, D_MODEL), BF16)],
        compiler_params=_params(2),
        name="mix_latent" if mod_stride else "mix_context",
    )(x, mod, attn, g, fcs, cs, gates, wa, wc, wf, wo, cw, cvec, dvec)


def _ffn_kernel(xp_ref, x_ref, xn_ref, mod_ref, wug_ref, wuv_ref, wd_ref, fw_ref, fb_ref, lvec_ref,
                o_ref, xs_ref, ext_ref, ug0_ref, uv0_ref, ug1_ref, uv1_ref, act_ref, y_ref, ys_ref,
                *, seq, tm):
    i = pl.program_id(0)
    r0 = i * tm
    n_slab = D_MODEL // LANES
    pitch = tm // SUBLANES + 2
    n_virt = SUBLANES * pitch
    scale = 1.0 + mod_ref[0, 4:5, :]
    shift = mod_ref[0, 3:4, :]
    gain = mod_ref[0, 5:6, :]
    n_chunk = D_FF // FFN_CHUNK
    row8 = lax.broadcasted_iota(jnp.int32, (SUBLANES, D_MODEL), 0)
    pieces = [(lo, min(lo + FFN_ROWS, n_virt)) for lo in range(0, n_virt, FFN_ROWS)]

    n_seg = max(1, tm // seq)
    if n_seg == 1:
        segments = [(0, 0, tm)]
    else:
        assert n_seg * (seq + SUBLANES) == n_virt
        segments = [(k * (seq + SUBLANES), k * seq, seq) for k in range(n_seg)]

    tail = jnp.concatenate([jnp.where(row8 == 0, xn_ref[...], 0.0),
                            jnp.where(row8 == SUBLANES - 1, xp_ref[...], 0.0)], axis=0)
    for c in range(n_slab):
        lanes = slice(c * LANES, (c + 1) * LANES)
        for virt, real, length in segments:
            xs_ref[c, virt:virt + length, :] = x_ref[real:real + length, lanes]
        if n_seg == 1:
            xs_ref[c, tm:n_virt, :] = tail[:, lanes]
        else:
            for virt, _, length in segments:
                xs_ref[c, virt + length:virt + length + SUBLANES, :] = jnp.zeros((SUBLANES, LANES), F32)

    def strided_rows(j):
        return jnp.concatenate(
            [xs_ref[c, pl.ds(j, SUBLANES, stride=pitch), :] for c in range(n_slab)], axis=1)

    no_next = (r0 + tm) % seq == 0
    no_prev = r0 % seq == 0
    k16 = lax.broadcasted_iota(jnp.int32, (2 * SUBLANES, 1), 0)
    for jj in range(0, pitch, 2):
        xj = jnp.concatenate([strided_rows(jj), strided_rows(jj + 1)], axis=0)
        virt_row = jj + k16 // SUBLANES + pitch * (k16 % SUBLANES)
        if n_seg == 1:
            padding = ((virt_row == tm) & no_next) | ((virt_row == n_virt - 1) & no_prev)
        else:
            padding = virt_row % (seq + SUBLANES) >= seq
        h = jnp.where(padding, 0.0, _ln(xj) * scale + shift)
        ext_ref[jj * SUBLANES:(jj + 2) * SUBLANES, :] = h.astype(BF16)

    u_refs = ((ug0_ref, uv0_ref), (ug1_ref, uv1_ref))

    def up(c):
        g_ref, v_ref = u_refs[c % 2]
        lo = c * FFN_CHUNK
        g_ref[...] = _dot(ext_ref[...], wug_ref[:, lo:lo + FFN_CHUNK])
        v_ref[...] = _dot(ext_ref[...], wuv_ref[:, lo:lo + FFN_CHUNK])

    def conv3(u_ref, lo, hi, col):
        c0 = col * FFN_CHUNK
        if lo >= SUBLANES:
            prev = u_ref[lo - SUBLANES:hi - SUBLANES, :]
        else:
            prev = jnp.concatenate([pltpu.roll(u_ref[n_virt - SUBLANES:n_virt, :], 1, 0),
                                    u_ref[0:hi - SUBLANES, :]], axis=0)
        if hi + SUBLANES <= n_virt:
            nxt = u_ref[lo + SUBLANES:hi + SUBLANES, :]
        else:
            nxt = jnp.concatenate([u_ref[lo + SUBLANES:n_virt, :],
                                   pltpu.roll(u_ref[0:SUBLANES, :], SUBLANES - 1, 0)], axis=0)

        def rows(w8):
            return jnp.concatenate([w8] * ((hi - lo) // SUBLANES), axis=0)

        cols = slice(c0, c0 + FFN_CHUNK)
        return (rows(fw_ref[0, :, cols]) * prev + rows(fw_ref[1, :, cols]) * u_ref[lo:hi, :]
                + rows(fw_ref[2, :, cols]) * nxt + rows(fb_ref[:, cols]))

    def activate(c):
        g_ref, v_ref = u_refs[c % 2]
        for lo, hi in pieces:
            yg = conv3(g_ref, lo, hi, c)
            yv = conv3(v_ref, lo, hi, c + n_chunk)
            act_ref[lo:hi, c * FFN_CHUNK:(c + 1) * FFN_CHUNK] = (yg * _sigmoid(yg) * yv).astype(BF16)

    up(0)
    for c in range(n_chunk):
        if c + 1 < n_chunk:
            up(c + 1)
        activate(c)
    y_ref[...] = _dot(act_ref[...], wd_ref[...])

    for jj in range(0, pitch, 2):
        xj = jnp.concatenate([strided_rows(jj), strided_rows(jj + 1)], axis=0)
        r = ALPHA * xj + gain * y_ref[jj * SUBLANES:(jj + 2) * SUBLANES, :]
        out = _ln(r) * lvec_ref[0:1, :] + lvec_ref[1:2, :]
        for k in range(2):
            for c in range(n_slab):
                ys_ref[c, pl.ds(jj + k, SUBLANES, stride=pitch), :] = (
                    out[k * SUBLANES:(k + 1) * SUBLANES, c * LANES:(c + 1) * LANES])
    for c in range(n_slab):
        for virt, real, length in segments:
            o_ref[real:real + length, c * LANES:(c + 1) * LANES] = ys_ref[c, virt:virt + length, :]


def _ffn(x, mod, layer, mod_off, mod_stride, seq, w_up, wd, fw, fb, lvec):
    n_tok = x.shape[0]
    tm = 512
    assert seq % tm == 0 or tm % seq == 0
    halo = tm // SUBLANES
    last = n_tok // SUBLANES - 1
    n_virt = tm + 2 * SUBLANES
    half = lambda j: pl.BlockSpec((None, D_MODEL, D_FF), lambda i: (layer, 0, j),
                                  pipeline_mode=pl.Buffered(1))
    return pl.pallas_call(
        functools.partial(_ffn_kernel, seq=seq, tm=tm),
        grid=(n_tok // tm,),
        in_specs=[pl.BlockSpec((SUBLANES, D_MODEL), lambda i: (jnp.maximum(i * halo - 1, 0), 0)),
                  pl.BlockSpec((tm, D_MODEL), lambda i: (i, 0)),
                  pl.BlockSpec((SUBLANES, D_MODEL), lambda i: (jnp.minimum((i + 1) * halo, last), 0)),
                  _mod_spec(layer, lambda i: mod_off + mod_stride * ((i * tm) // seq)),
                  half(0),
                  half(1),
                  _resident((D_FF, D_MODEL), layer),
                  _resident((3, SUBLANES, 2 * D_FF), layer),
                  _resident((SUBLANES, 2 * D_FF), layer),
                  _resident((2, D_MODEL), layer)],
        out_specs=pl.BlockSpec((tm, D_MODEL), lambda i: (i, 0)),
        out_shape=jax.ShapeDtypeStruct((n_tok, D_MODEL), F32),
        scratch_shapes=[pltpu.VMEM((D_MODEL // LANES, n_virt, LANES), F32),
                        pltpu.VMEM((n_virt, D_MODEL), BF16)]
        + [pltpu.VMEM((n_virt, FFN_CHUNK), F32)] * 4
        + [pltpu.VMEM((n_virt, D_FF), BF16), pltpu.VMEM((n_virt, D_MODEL), F32),
           pltpu.VMEM((D_MODEL // LANES, n_virt, LANES), F32)],
        compiler_params=_params(1),
        name="ffn_latent" if mod_stride else "ffn_context",
    )(x, x, x, mod, w_up, w_up, wd, fw, fb, lvec)


def _dft_cos_sin(n):
    idx = np.arange(n, dtype=np.int64)
    ang = 2.0 * np.pi * ((idx[:, None] * idx[None, :]) % n) / n
    return np.cos(ang) / np.sqrt(n), np.sin(ang) / np.sqrt(n)


def _channel_dft_table():
    c, s = _dft_cos_sin(FNET_GC)
    out = np.zeros((FNET_CH, 2 * FNET_CH), np.float32)
    for gi in range(FNET_GROUPS):
        sl = slice(gi * FNET_GC, (gi + 1) * FNET_GC)
        out[sl, sl] = c
        out[sl, FNET_CH + gi * FNET_GC:FNET_CH + (gi + 1) * FNET_GC] = s
    return jnp.asarray(out, dtype=BF16)


def _position_dft_table(n):
    c, s = _dft_cos_sin(n)
    return jnp.asarray(np.concatenate([c, -s], axis=1).astype(np.float32), dtype=BF16)


def _rope_tables(length):
    n_rows = length // GRID_W
    row = jnp.repeat(jnp.arange(n_rows, dtype=F32), GRID_W)
    col = jnp.tile(jnp.arange(GRID_W, dtype=F32), n_rows)
    quarter = HEAD_DIM // 4
    inv = ROPE_THETA ** (-jnp.arange(quarter, dtype=F32) / quarter)
    ang = jnp.concatenate([row[:, None] * inv, col[:, None] * inv], axis=-1)
    cos, sin = jnp.cos(ang), jnp.sin(ang)
    reps = LANES // HEAD_DIM
    return (jnp.tile(jnp.concatenate([cos, cos], axis=-1), (1, reps)),
            jnp.tile(jnp.concatenate([-sin, sin], axis=-1), (1, reps)))


def kernel(x_prompt, x_sample, cache_k, cache_v, c, c_ctx, w_mod, b_mod, w_in, b_in, sink,
           w_attn_o, conv_w, conv_b, conv_ln_g, conv_ln_b, w_conv_o, w_fnet, b_fnet, w_o,
           ln1_g, ln1_b, w_up, ffn_conv_w, ffn_conv_b, w_down, ln2_g, ln2_b):
    batch, seq_p, _ = x_prompt.shape
    dec_batch, seq_s, _ = x_sample.shape
    past = cache_k.shape[2]
    assert 1 + dec_batch <= MOD_ROWS

    cc = jnp.concatenate([c_ctx[None, :], c], axis=0)
    cc = jnp.pad(cc, ((0, MOD_ROWS - cc.shape[0]), (0, 0)))
    mods = _modulation(cc, w_mod, b_mod).reshape(DEPTH, MOD_ROWS, 6, D_MODEL)

    bcs = _channel_dft_table()
    cs_p = _position_dft_table(seq_p)
    cs_s = _position_dft_table(seq_s)
    rope = _rope_tables(seq_s)
    ck = cache_k.reshape(dec_batch, DEPTH, past, KV_W)
    cv = cache_v.reshape(dec_batch, DEPTH, past, KV_W)

    w_in_b, wa, wc, wf, wo, w_up_b, wd = (
        w.astype(BF16) for w in (w_in, w_attn_o, w_conv_o, w_fnet, w_o, w_up, w_down))
    b_in3 = b_in[:, None, :]
    cw = jnp.broadcast_to(conv_w[:, :, None, :], (DEPTH, CONV_K, SUBLANES, CONV_CH))
    cvec = jnp.stack([conv_b, conv_ln_g, conv_ln_b], axis=1)
    dvec = jnp.stack([b_fnet, ln1_g, ln1_b], axis=1)
    fw = jnp.broadcast_to(ffn_conv_w[:, :, None, :], (DEPTH, 3, SUBLANES, 2 * D_FF))
    fb = jnp.broadcast_to(ffn_conv_b[:, None, :], (DEPTH, SUBLANES, 2 * D_FF))
    lvec = jnp.stack([ln2_g, ln2_b], axis=1)

    xp = x_prompt.reshape(batch * seq_p, D_MODEL)
    xs = x_sample.reshape(dec_batch * seq_s, D_MODEL)
    new_k, new_v = [], []
    for l in range(DEPTH):
        def run(x, mod_off, mod_stride, seq, rope_tabs, attend, cs):
            q, k, v, g, fcs, gates = _inproj(x, mods, l, mod_off, mod_stride, seq, w_in_b, b_in3, bcs,
                                             rope_tabs)
            attn = attend(q, k, v)
            x = _mix(x, mods, l, mod_off, mod_stride, seq, attn, g, fcs, cs, gates, wa, wc, wf, wo,
                     cw, cvec, dvec)
            x = _ffn(x, mods, l, mod_off, mod_stride, seq, w_up_b, wd, fw, fb, lvec)
            return x, k, v

        xp, kp, vp = run(xp, 0, 0, seq_p, None,
                         lambda q, k, v: _ctx_attn(q, k, v, sink[l], seq_p), cs_p)
        new_k.append(kp.reshape(batch, seq_p, N_KV_HEADS, HEAD_DIM))
        new_v.append(vp.reshape(batch, seq_p, N_KV_HEADS, HEAD_DIM))
        xs, _, _ = run(xs, 1, 1, seq_s, rope,
                       lambda q, k, v: _win_attn(q, k, v, ck, cv, l, sink[l], seq_s), cs_s)

    return (xp.reshape(batch, seq_p, D_MODEL), xs.reshape(dec_batch, seq_s, D_MODEL),
            jnp.stack(new_k, axis=1), jnp.stack(new_v, axis=1))
```

```python
import functools
import math

import jax
import jax.numpy as jnp
import numpy as np
from jax import lax
from jax.experimental import pallas as pl
from jax.experimental.pallas import tpu as pltpu

D_MODEL = 1024
DEPTH = 2
GRID_W = 64
N_HEADS = 8
N_KV_HEADS = 2
HEAD_DIM = 64
ATTN_W = N_HEADS * HEAD_DIM
KV_W = N_KV_HEADS * HEAD_DIM
WINDOW = 128
BLOCK = 128
WIN_QBLOCKS = 4
CTX_SEQS = 4
CONV_CH = D_MODEL // 4
CONV_K = 31
CONV_PAD = 16
FNET_GROUPS = 4
FNET_CH = D_MODEL // 4
FNET_GC = FNET_CH // FNET_GROUPS
N_GATES = 3 * D_MODEL
N_IN = ATTN_W + 2 * KV_W + 2 * CONV_CH + FNET_CH + N_GATES
D_FF = int(math.ceil(8 * D_MODEL / 3 / 128)) * 128
FFN_CHUNK = 256
FFN_ROWS = 48
ROPE_THETA = 10000.0
ALPHA = (2 * DEPTH) ** 0.25
NEG_INF = -1e30
LN_EPS = 1e-6

OFF_K = ATTN_W
OFF_V = OFF_K + KV_W
OFF_CA = OFF_V + KV_W
OFF_CB = OFF_CA + CONV_CH
OFF_F = OFF_CB + CONV_CH
OFF_G = OFF_F + FNET_CH

LANES = 128
SUBLANES = 8
VMEM_LIMIT = 56 * 1024 * 1024
MOD_ROWS = 16
ROW_PIECE = 32

F32 = jnp.float32
BF16 = jnp.bfloat16


def _resident(shape, layer=None):
    nd = len(shape)
    if layer is None:
        return pl.BlockSpec(shape, lambda *_: (0,) * nd, pipeline_mode=pl.Buffered(1))
    return pl.BlockSpec((None,) + tuple(shape), lambda *_: (layer,) + (0,) * nd,
                        pipeline_mode=pl.Buffered(1))


def _mod_spec(layer, row_of):
    return pl.BlockSpec((None, 1, 6, D_MODEL), lambda *idx: (layer, row_of(*idx), 0, 0))


def _params(n_axes):
    return pltpu.CompilerParams(dimension_semantics=("arbitrary",) * n_axes,
                                vmem_limit_bytes=VMEM_LIMIT)


def _sigmoid(x):
    return 1.0 / (1.0 + jnp.exp(-x))


def _ln(x):
    mu = jnp.mean(x, axis=-1, keepdims=True)
    xc = x - mu
    var = jnp.mean(xc * xc, axis=-1, keepdims=True)
    return xc * lax.rsqrt(var + LN_EPS)


def _dot(a, b):
    return jnp.dot(a, b, preferred_element_type=F32)


def _mod_kernel(c_ref, w_ref, b_ref, o_ref):
    c = c_ref[...]
    a = (c * _sigmoid(c)).astype(BF16)
    o_ref[0] = _dot(a, w_ref[0].astype(BF16)) + b_ref[0]


def _modulation(cc, w_mod, b_mod):
    tn = 1536
    n_out = w_mod.shape[-1]
    return pl.pallas_call(
        _mod_kernel,
        grid=(DEPTH, n_out // tn),
        in_specs=[pl.BlockSpec((MOD_ROWS, D_MODEL), lambda l, j: (0, 0)),
                  pl.BlockSpec((1, D_MODEL, tn), lambda l, j: (l, 0, j)),
                  pl.BlockSpec((1, 1, tn), lambda l, j: (l, 0, j))],
        out_specs=pl.BlockSpec((1, MOD_ROWS, tn), lambda l, j: (l, 0, j)),
        out_shape=jax.ShapeDtypeStruct((DEPTH, MOD_ROWS, n_out), F32),
        compiler_params=_params(2),
        name="modulation",
    )(cc, w_mod, b_mod.reshape(DEPTH, 1, n_out))


def _rope_tile(t, cos, sin):
    lane = lax.broadcasted_iota(jnp.int32, t.shape, 1)
    first_half = (lane % HEAD_DIM) < (HEAD_DIM // 2)
    partner = jnp.where(first_half, pltpu.roll(t, LANES - HEAD_DIM // 2, 1),
                        pltpu.roll(t, HEAD_DIM // 2, 1))
    return t * cos + partner * sin


def _inproj_kernel(*refs, latent):
    if latent:
        (x_ref, mod_ref, w_ref, b_ref, bcs_ref, cos_ref, sin_ref,
         q_ref, k_ref, v_ref, g_ref, fcs_ref, gate_ref, hb_ref, z_ref) = refs
    else:
        (x_ref, mod_ref, w_ref, b_ref, bcs_ref,
         q_ref, k_ref, v_ref, g_ref, fcs_ref, gate_ref, hb_ref, z_ref) = refs
    tm = x_ref.shape[0]
    scale = 1.0 + mod_ref[0, 1:2, :]
    shift = mod_ref[0, 0:1, :]
    for rp in range(tm // ROW_PIECE):
        rows = slice(rp * ROW_PIECE, (rp + 1) * ROW_PIECE)
        hb_ref[rows, :] = (_ln(x_ref[rows, :]) * scale + shift).astype(BF16)

    z_ref[...] = _dot(hb_ref[...], w_ref[:, 0:OFF_G]) + b_ref[:, 0:OFF_G]
    for t in range(3):
        c0 = OFF_G + t * D_MODEL
        gate_ref[:, t * D_MODEL:(t + 1) * D_MODEL] = _sigmoid(
            _dot(hb_ref[...], w_ref[:, c0:c0 + D_MODEL]) + b_ref[:, c0:c0 + D_MODEL]).astype(BF16)

    head_scale = HEAD_DIM ** -0.5
    for t in range(ATTN_W // LANES):
        qt = z_ref[:, t * LANES:(t + 1) * LANES]
        if latent:
            qt = _rope_tile(qt, cos_ref[...], sin_ref[...])
        q_ref[:, t * LANES:(t + 1) * LANES] = (qt * head_scale).astype(BF16)
    kt = z_ref[:, OFF_K:OFF_V]
    if latent:
        kt = _rope_tile(kt, cos_ref[...], sin_ref[...])
    k_ref[...] = kt
    v_ref[...] = z_ref[:, OFF_V:OFF_CA]
    g_ref[...] = z_ref[:, OFF_CA:OFF_CB] * _sigmoid(z_ref[:, OFF_CB:OFF_F])
    fcs_ref[...] = _dot(z_ref[:, OFF_F:OFF_G].astype(BF16), bcs_ref[...]).astype(BF16)


def _inproj(x, mod, layer, mod_off, mod_stride, seq, w_in, b_in, bcs, rope):
    tm = 256
    n_tok = x.shape[0]
    latent = rope is not None
    tok = lambda i: (i, 0)
    in_specs = [pl.BlockSpec((tm, D_MODEL), tok),
                _mod_spec(layer, lambda i: mod_off + mod_stride * ((i * tm) // seq)),
                _resident((D_MODEL, N_IN), layer),
                _resident((1, N_IN), layer),
                _resident((FNET_CH, 2 * FNET_CH))]
    args = [x, mod, w_in, b_in, bcs]
    if latent:
        pos = lambda i: (i % (seq // tm), 0)
        in_specs += [pl.BlockSpec((tm, LANES), pos), pl.BlockSpec((tm, LANES), pos)]
        args += list(rope)
    widths = (ATTN_W, KV_W, KV_W, CONV_CH, 2 * FNET_CH, N_GATES)
    dtypes = (BF16, F32, F32, F32, BF16, BF16)
    return pl.pallas_call(
        functools.partial(_inproj_kernel, latent=latent),
        grid=(n_tok // tm,),
        in_specs=in_specs,
        out_specs=[pl.BlockSpec((tm, w), tok) for w in widths],
        out_shape=[jax.ShapeDtypeStruct((n_tok, w), d) for w, d in zip(widths, dtypes)],
        scratch_shapes=[pltpu.VMEM((tm, D_MODEL), BF16), pltpu.VMEM((tm, OFF_G), F32)],
        compiler_params=_params(1),
        name="inproj_latent" if latent else "inproj_context",
    )(*args)


def _pair_halves(x, kv):
    lane = lax.broadcasted_iota(jnp.int32, x.shape, 1)
    low = lane < HEAD_DIM
    swapped = pltpu.roll(x, HEAD_DIM, 1)
    lo_src, hi_src = (x, swapped) if kv == 0 else (swapped, x)
    return jnp.where(low, lo_src, 0.0).astype(BF16), jnp.where(low, 0.0, hi_src).astype(BF16)


def _softmax_terms(s, sink):
    m = jnp.maximum(jnp.max(s, axis=-1, keepdims=True), sink)
    p = jnp.exp(s - m)
    denom = jnp.sum(p, axis=-1, keepdims=True) + jnp.exp(sink - m)
    return p, denom


def _attend_group(q_ref, o_ref, sink_ref, kv, kb, vb, mask, rows=slice(None)):
    n = kb.shape[0] // 2
    t0 = kv * (N_HEADS // N_KV_HEADS) // 2
    q2 = jnp.concatenate([q_ref[rows, t0 * LANES:(t0 + 1) * LANES],
                          q_ref[rows, (t0 + 1) * LANES:(t0 + 2) * LANES]], axis=0)
    tq = q2.shape[0] // 2
    s = lax.dot_general(q2, kb, (((1,), (1,)), ((), ())), preferred_element_type=F32)
    first_tile = lax.broadcasted_iota(jnp.int32, (2 * tq, 1), 0) < tq
    parts, inv = [], []
    for half in range(2):
        sh = s[:, half * n:(half + 1) * n]
        if mask is not None:
            nm = mask.shape[1]
            sh = jnp.concatenate([jnp.where(mask, sh[:, :nm], NEG_INF), sh[:, nm:]], axis=1)
        head = 2 * t0 + half
        sink = jnp.where(first_tile, sink_ref[head], sink_ref[head + 2])
        p, denom = _softmax_terms(sh, sink)
        parts.append(p.astype(BF16))
        inv.append(1.0 / denom)
    o = _dot(jnp.concatenate(parts, axis=1), vb)
    lane = lax.broadcasted_iota(jnp.int32, o.shape, 1)
    o = (o * jnp.where(lane < HEAD_DIM, inv[0], inv[1])).astype(o_ref.dtype)
    o_ref[rows, t0 * LANES:(t0 + 1) * LANES] = o[:tq]
    o_ref[rows, (t0 + 1) * LANES:(t0 + 2) * LANES] = o[tq:]


def _ctx_attn_kernel(sink_ref, q_ref, k_ref, v_ref, o_ref, *, seq):
    for sb in range(CTX_SEQS):
        rows = slice(sb * seq, (sb + 1) * seq)
        for kv in range(N_KV_HEADS):
            kb = jnp.concatenate(_pair_halves(k_ref[rows, :], kv), axis=0)
            vb = jnp.concatenate(_pair_halves(v_ref[rows, :], kv), axis=0)
            _attend_group(q_ref, o_ref, sink_ref, kv, kb, vb, None, rows=rows)


def _ctx_attn(q, k, v, sink, seq):
    n_tok = q.shape[0]
    tok = lambda b: (b, 0)
    rows = CTX_SEQS * seq
    assert n_tok % rows == 0
    return pl.pallas_call(
        functools.partial(_ctx_attn_kernel, seq=seq),
        grid=(n_tok // rows,),
        in_specs=[pl.BlockSpec(memory_space=pltpu.SMEM),
                  pl.BlockSpec((rows, ATTN_W), tok),
                  pl.BlockSpec((rows, KV_W), tok),
                  pl.BlockSpec((rows, KV_W), tok)],
        out_specs=pl.BlockSpec((rows, ATTN_W), tok),
        out_shape=jax.ShapeDtypeStruct((n_tok, ATTN_W), BF16),
        compiler_params=_params(1),
        name="context_attention",
    )(sink, q, k, v)


def _win_attn_kernel(sink_ref, q_ref, k_ref, v_ref, ck_ref, cv_ref, o_ref,
                     kl_ref, vl_ref, kc_ref, vc_ref, *, seq):
    step = pl.program_id(1)

    @pl.when(step == 0)
    def _():
        for kv in range(N_KV_HEADS):
            for src, dst in ((k_ref[...], kl_ref), (v_ref[...], vl_ref),
                             (ck_ref[0, 0], kc_ref), (cv_ref[0, 0], vc_ref)):
                lo, hi = _pair_halves(src, kv)
                dst[kv, 0] = lo
                dst[kv, 1] = hi

    span = 3 * BLOCK
    for sb in range(WIN_QBLOCKS):
        qb = step * WIN_QBLOCKS + sb
        start = pl.multiple_of(jnp.clip(qb * BLOCK - BLOCK, 0, seq - span), BLOCK)
        qpos = qb * BLOCK + lax.broadcasted_iota(jnp.int32, (BLOCK, span), 0)
        kpos = start + lax.broadcasted_iota(jnp.int32, (BLOCK, span), 1)
        mask = jnp.abs(qpos - kpos) <= WINDOW
        mask = jnp.concatenate([mask, mask], axis=0)
        for kv in range(N_KV_HEADS):
            kb = jnp.concatenate([kl_ref[kv, 0, pl.ds(start, span), :], kc_ref[kv, 0],
                                  kl_ref[kv, 1, pl.ds(start, span), :], kc_ref[kv, 1]], axis=0)
            vb = jnp.concatenate([vl_ref[kv, 0, pl.ds(start, span), :], vc_ref[kv, 0],
                                  vl_ref[kv, 1, pl.ds(start, span), :], vc_ref[kv, 1]], axis=0)
            _attend_group(q_ref, o_ref, sink_ref, kv, kb, vb, mask,
                          rows=slice(sb * BLOCK, (sb + 1) * BLOCK))


def _win_attn(q, k, v, cache_k, cache_v, layer, sink, seq):
    n_tok = q.shape[0]
    bsz = n_tok // seq
    nb = seq // (BLOCK * WIN_QBLOCKS)
    past = cache_k.shape[2]
    qmap = lambda b, i: (b * nb + i, 0)
    kvmap = lambda b, i: (b, 0)
    cmap = lambda b, i: (b, layer, 0, 0)
    return pl.pallas_call(
        functools.partial(_win_attn_kernel, seq=seq),
        grid=(bsz, nb),
        in_specs=[pl.BlockSpec(memory_space=pltpu.SMEM),
                  pl.BlockSpec((WIN_QBLOCKS * BLOCK, ATTN_W), qmap),
                  pl.BlockSpec((seq, KV_W), kvmap),
                  pl.BlockSpec((seq, KV_W), kvmap),
                  pl.BlockSpec((1, 1, past, KV_W), cmap),
                  pl.BlockSpec((1, 1, past, KV_W), cmap)],
        out_specs=pl.BlockSpec((WIN_QBLOCKS * BLOCK, ATTN_W), qmap),
        out_shape=jax.ShapeDtypeStruct((n_tok, ATTN_W), BF16),
        scratch_shapes=[pltpu.VMEM((N_KV_HEADS, 2, seq, KV_W), BF16)] * 2
        + [pltpu.VMEM((N_KV_HEADS, 2, past, KV_W), BF16)] * 2,
        compiler_params=_params(2),
        name="window_attention",
    )(sink, q, k, v, cache_k, cache_v)


CONV_ROWS = 32


def _mix_kernel(x_ref, mod_ref, attn_ref, g_ref, fcs_ref, cs_ref, gate_ref,
                wa_ref, wc_ref, wf_ref, wo_ref, cw_ref, cvec_ref, dvec_ref,
                o_ref, gp_ref, gs_ref, u_ref, m_ref, mb_ref, *, seq, tm):
    i = pl.program_id(1)
    nblk = seq // tm
    r0 = pl.multiple_of(i * tm, tm)

    gp_ref[CONV_PAD:CONV_PAD + tm, :] = g_ref[pl.ds(r0, tm), :]
    top = g_ref[pl.ds(pl.multiple_of(jnp.maximum(r0 - CONV_PAD, 0), SUBLANES), CONV_PAD), :]
    gp_ref[0:CONV_PAD, :] = jnp.where(i > 0, top, 0.0)
    bot = g_ref[pl.ds(pl.multiple_of(jnp.minimum(r0 + tm, seq - CONV_PAD), SUBLANES), CONV_PAD), :]
    gp_ref[CONV_PAD + tm:, :] = jnp.where(i < nblk - 1, bot, 0.0)
    n_pad = tm + 2 * CONV_PAD
    staged = gp_ref[...]
    for r in range(1, SUBLANES):
        gs_ref[r - 1] = pltpu.roll(staged, n_pad - r, 0)
    first = CONV_PAD - CONV_K // 2
    n_tile = CONV_ROWS // SUBLANES
    for rc in range(tm // CONV_ROWS):
        accs = [None] * n_tile
        for j in range(CONV_K):
            a, r = divmod(first + j, SUBLANES)
            src = gp_ref if r == 0 else gs_ref.at[r - 1]
            w8 = cw_ref[j]
            for t in range(n_tile):
                lo = rc * CONV_ROWS + (a + t) * SUBLANES
                term = w8 * src[lo:lo + SUBLANES, :]
                accs[t] = term if accs[t] is None else accs[t] + term
        acc = jnp.concatenate(accs, axis=0)
        y = _ln(acc + cvec_ref[0:1, :]) * cvec_ref[1:2, :] + cvec_ref[2:3, :]
        u_ref[rc * CONV_ROWS:(rc + 1) * CONV_ROWS, :] = (y * _sigmoid(y)).astype(BF16)

    fm = (_dot(cs_ref[pl.ds(r0, tm), 0:seq], fcs_ref[:, 0:FNET_CH])
          + _dot(cs_ref[pl.ds(r0, tm), seq:], fcs_ref[:, FNET_CH:]))

    m_ref[...] = gate_ref[:, 0:D_MODEL].astype(F32) * _dot(attn_ref[...], wa_ref[...])
    m_ref[...] += gate_ref[:, D_MODEL:2 * D_MODEL].astype(F32) * _dot(u_ref[...], wc_ref[...])
    mb_ref[...] = (m_ref[...] + gate_ref[:, 2 * D_MODEL:].astype(F32)
                   * (_dot(fm.astype(BF16), wf_ref[...]) + dvec_ref[0:1, :])).astype(BF16)
    m_ref[...] = _dot(mb_ref[...], wo_ref[...])
    gain = mod_ref[0, 2:3, :]
    for rp in range(tm // ROW_PIECE):
        rows = slice(rp * ROW_PIECE, (rp + 1) * ROW_PIECE)
        r = ALPHA * x_ref[rows, :] + gain * m_ref[rows, :]
        o_ref[rows, :] = _ln(r) * dvec_ref[1:2, :] + dvec_ref[2:3, :]


def _mix(x, mod, layer, mod_off, mod_stride, seq, attn, g, fcs, cs, gates, wa, wc, wf, wo, cw, cvec,
         dvec):
    n_tok = x.shape[0]
    tm = min(512, seq)
    bsz = n_tok // seq
    nblk = seq // tm
    tok = lambda b, i: (b * nblk + i, 0)
    per_seq = lambda b, i: (b, 0)
    return pl.pallas_call(
        functools.partial(_mix_kernel, seq=seq, tm=tm),
        grid=(bsz, nblk),
        in_specs=[pl.BlockSpec((tm, D_MODEL), tok),
                  _mod_spec(layer, lambda b, i: mod_off + mod_stride * b),
                  pl.BlockSpec((tm, ATTN_W), tok),
                  pl.BlockSpec((seq, CONV_CH), per_seq),
                  pl.BlockSpec((seq, 2 * FNET_CH), per_seq),
                  _resident((seq, 2 * seq)),
                  pl.BlockSpec((tm, N_GATES), tok),
                  _resident((ATTN_W, D_MODEL), layer),
                  _resident((CONV_CH, D_MODEL), layer),
                  _resident((FNET_CH, D_MODEL), layer),
                  _resident((D_MODEL, D_MODEL), layer),
                  _resident((CONV_K, SUBLANES, CONV_CH), layer),
                  _resident((3, CONV_CH), layer),
                  _resident((3, D_MODEL), layer)],
        out_specs=pl.BlockSpec((tm, D_MODEL), tok),
        out_shape=jax.ShapeDtypeStruct((n_tok, D_MODEL), F32),
        scratch_shapes=[pltpu.VMEM((tm + 2 * CONV_PAD, CONV_CH), F32),
                        pltpu.VMEM((SUBLANES - 1, tm + 2 * CONV_PAD, CONV_CH), F32),
                        pltpu.VMEM((tm, CONV_CH), BF16),
                        pltpu.VMEM((tm, D_MODEL), F32),
                        pltpu.VMEM((tm, D_MODEL), BF16)],
        compiler_params=_params(2),
        name="mix_latent" if mod_stride else "mix_context",
    )(x, mod, attn, g, fcs, cs, gates, wa, wc, wf, wo, cw, cvec, dvec)


def _ffn_kernel(xp_ref, x_ref, xn_ref, mod_ref, wug_ref, wuv_ref, wd_ref, fw_ref, fb_ref, lvec_ref,
                o_ref, xs_ref, ext_ref, ug0_ref, uv0_ref, ug1_ref, uv1_ref, act_ref, y_ref, ys_ref,
                *, seq, tm):
    i = pl.program_id(0)
    r0 = i * tm
    n_slab = D_MODEL // LANES
    pitch = tm // SUBLANES + 2
    n_virt = SUBLANES * pitch
    scale = 1.0 + mod_ref[0, 4:5, :]
    shift = mod_ref[0, 3:4, :]
    gain = mod_ref[0, 5:6, :]
    n_chunk = D_FF // FFN_CHUNK
    row8 = lax.broadcasted_iota(jnp.int32, (SUBLANES, D_MODEL), 0)
    pieces = [(lo, min(lo + FFN_ROWS, n_virt)) for lo in range(0, n_virt, FFN_ROWS)]

    n_seg = max(1, tm // seq)
    if n_seg == 1:
        segments = [(0, 0, tm)]
    else:
        assert n_seg * (seq + SUBLANES) == n_virt
        segments = [(k * (seq + SUBLANES), k * seq, seq) for k in range(n_seg)]

    tail = jnp.concatenate([jnp.where(row8 == 0, xn_ref[...], 0.0),
                            jnp.where(row8 == SUBLANES - 1, xp_ref[...], 0.0)], axis=0)
    for c in range(n_slab):
        lanes = slice(c * LANES, (c + 1) * LANES)
        for virt, real, length in segments:
            xs_ref[c, virt:virt + length, :] = x_ref[real:real + length, lanes]
        if n_seg == 1:
            xs_ref[c, tm:n_virt, :] = tail[:, lanes]
        else:
            for virt, _, length in segments:
                xs_ref[c, virt + length:virt + length + SUBLANES, :] = jnp.zeros((SUBLANES, LANES), F32)

    def strided_rows(j):
        return jnp.concatenate(
            [xs_ref[c, pl.ds(j, SUBLANES, stride=pitch), :] for c in range(n_slab)], axis=1)

    no_next = (r0 + tm) % seq == 0
    no_prev = r0 % seq == 0
    k16 = lax.broadcasted_iota(jnp.int32, (2 * SUBLANES, 1), 0)
    for jj in range(0, pitch, 2):
        xj = jnp.concatenate([strided_rows(jj), strided_rows(jj + 1)], axis=0)
        virt_row = jj + k16 // SUBLANES + pitch * (k16 % SUBLANES)
        if n_seg == 1:
            padding = ((virt_row == tm) & no_next) | ((virt_row == n_virt - 1) & no_prev)
        else:
            padding = virt_row % (seq + SUBLANES) >= seq
        h = jnp.where(padding, 0.0, _ln(xj) * scale + shift)
        ext_ref[jj * SUBLANES:(jj + 2) * SUBLANES, :] = h.astype(BF16)

    u_refs = ((ug0_ref, uv0_ref), (ug1_ref, uv1_ref))

    def up(c):
        g_ref, v_ref = u_refs[c % 2]
        lo = c * FFN_CHUNK
        g_ref[...] = _dot(ext_ref[...], wug_ref[:, lo:lo + FFN_CHUNK])
        v_ref[...] = _dot(ext_ref[...], wuv_ref[:, lo:lo + FFN_CHUNK])

    def conv3(u_ref, lo, hi, col):
        c0 = col * FFN_CHUNK
        if lo >= SUBLANES:
            prev = u_ref[lo - SUBLANES:hi - SUBLANES, :]
        else:
            prev = jnp.concatenate([pltpu.roll(u_ref[n_virt - SUBLANES:n_virt, :], 1, 0),
                                    u_ref[0:hi - SUBLANES, :]], axis=0)
        if hi + SUBLANES <= n_virt:
            nxt = u_ref[lo + SUBLANES:hi + SUBLANES, :]
        else:
            nxt = jnp.concatenate([u_ref[lo + SUBLANES:n_virt, :],
                                   pltpu.roll(u_ref[0:SUBLANES, :], SUBLANES - 1, 0)], axis=0)

        def rows(w8):
            return jnp.concatenate([w8] * ((hi - lo) // SUBLANES), axis=0)

        cols = slice(c0, c0 + FFN_CHUNK)
        return (rows(fw_ref[0, :, cols]) * prev + rows(fw_ref[1, :, cols]) * u_ref[lo:hi, :]
                + rows(fw_ref[2, :, cols]) * nxt + rows(fb_ref[:, cols]))

    def activate(c):
        g_ref, v_ref = u_refs[c % 2]
        for lo, hi in pieces:
            yg = conv3(g_ref, lo, hi, c)
            yv = conv3(v_ref, lo, hi, c + n_chunk)
            act_ref[lo:hi, c * FFN_CHUNK:(c + 1) * FFN_CHUNK] = (yg * _sigmoid(yg) * yv).astype(BF16)

    up(0)
    for c in range(n_chunk):
        if c + 1 < n_chunk:
            up(c + 1)
        activate(c)
    y_ref[...] = _dot(act_ref[...], wd_ref[...])

    for jj in range(0, pitch, 2):
        xj = jnp.concatenate([strided_rows(jj), strided_rows(jj + 1)], axis=0)
        r = ALPHA * xj + gain * y_ref[jj * SUBLANES:(jj + 2) * SUBLANES, :]
        out = _ln(r) * lvec_ref[0:1, :] + lvec_ref[1:2, :]
        for k in range(2):
            for c in range(n_slab):
                ys_ref[c, pl.ds(jj + k, SUBLANES, stride=pitch), :] = (
                    out[k * SUBLANES:(k + 1) * SUBLANES, c * LANES:(c + 1) * LANES])
    for c in range(n_slab):
        for virt, real, length in segments:
            o_ref[real:real + length, c * LANES:(c + 1) * LANES] = ys_ref[c, virt:virt + length, :]


def _ffn(x, mod, layer, mod_off, mod_stride, seq, w_up, wd, fw, fb, lvec):
    n_tok = x.shape[0]
    tm = 512
    assert seq % tm == 0 or tm % seq == 0
    halo = tm // SUBLANES
    last = n_tok // SUBLANES - 1
    n_virt = tm + 2 * SUBLANES
    half = lambda j: pl.BlockSpec((None, D_MODEL, D_FF), lambda i: (layer, 0, j),
                                  pipeline_mode=pl.Buffered(1))
    return pl.pallas_call(
        functools.partial(_ffn_kernel, seq=seq, tm=tm),
        grid=(n_tok // tm,),
        in_specs=[pl.BlockSpec((SUBLANES, D_MODEL), lambda i: (jnp.maximum(i * halo - 1, 0), 0)),
                  pl.BlockSpec((tm, D_MODEL), lambda i: (i, 0)),
                  pl.BlockSpec((SUBLANES, D_MODEL), lambda i: (jnp.minimum((i + 1) * halo, last), 0)),
                  _mod_spec(layer, lambda i: mod_off + mod_stride * ((i * tm) // seq)),
                  half(0),
                  half(1),
                  _resident((D_FF, D_MODEL), layer),
                  _resident((3, SUBLANES, 2 * D_FF), layer),
                  _resident((SUBLANES, 2 * D_FF), layer),
                  _resident((2, D_MODEL), layer)],
        out_specs=pl.BlockSpec((tm, D_MODEL), lambda i: (i, 0)),
        out_shape=jax.ShapeDtypeStruct((n_tok, D_MODEL), F32),
        scratch_shapes=[pltpu.VMEM((D_MODEL // LANES, n_virt, LANES), F32),
                        pltpu.VMEM((n_virt, D_MODEL), BF16)]
        + [pltpu.VMEM((n_virt, FFN_CHUNK), F32)] * 4
        + [pltpu.VMEM((n_virt, D_FF), BF16), pltpu.VMEM((n_virt, D_MODEL), F32),
           pltpu.VMEM((D_MODEL // LANES, n_virt, LANES), F32)],
        compiler_params=_params(1),
        name="ffn_latent" if mod_stride else "ffn_context",
    )(x, x, x, mod, w_up, w_up, wd, fw, fb, lvec)


def _dft_cos_sin(n):
    idx = np.arange(n, dtype=np.int64)
    ang = 2.0 * np.pi * ((idx[:, None] * idx[None, :]) % n) / n
    return np.cos(ang) / np.sqrt(n), np.sin(ang) / np.sqrt(n)


def _channel_dft_table():
    c, s = _dft_cos_sin(FNET_GC)
    out = np.zeros((FNET_CH, 2 * FNET_CH), np.float32)
    for gi in range(FNET_GROUPS):
        sl = slice(gi * FNET_GC, (gi + 1) * FNET_GC)
        out[sl, sl] = c
        out[sl, FNET_CH + gi * FNET_GC:FNET_CH + (gi + 1) * FNET_GC] = s
    return jnp.asarray(out, dtype=BF16)


def _position_dft_table(n):
    c, s = _dft_cos_sin(n)
    return jnp.asarray(np.concatenate([c, -s], axis=1).astype(np.float32), dtype=BF16)


def _rope_tables(length):
    n_rows = length // GRID_W
    row = jnp.repeat(jnp.arange(n_rows, dtype=F32), GRID_W)
    col = jnp.tile(jnp.arange(GRID_W, dtype=F32), n_rows)
    quarter = HEAD_DIM // 4
    inv = ROPE_THETA ** (-jnp.arange(quarter, dtype=F32) / quarter)
    ang = jnp.concatenate([row[:, None] * inv, col[:, None] * inv], axis=-1)
    cos, sin = jnp.cos(ang), jnp.sin(ang)
    reps = LANES // HEAD_DIM
    return (jnp.tile(jnp.concatenate([cos, cos], axis=-1), (1, reps)),
            jnp.tile(jnp.concatenate([-sin, sin], axis=-1), (1, reps)))


def kernel(x_prompt, x_sample, cache_k, cache_v, c, c_ctx, w_mod, b_mod, w_in, b_in, sink,
           w_attn_o, conv_w, conv_b, conv_ln_g, conv_ln_b, w_conv_o, w_fnet, b_fnet, w_o,
           ln1_g, ln1_b, w_up, ffn_conv_w, ffn_conv_b, w_down, ln2_g, ln2_b):
    batch, seq_p, _ = x_prompt.shape
    dec_batch, seq_s, _ = x_sample.shape
    past = cache_k.shape[2]
    assert 1 + dec_batch <= MOD_ROWS

    cc = jnp.concatenate([c_ctx[None, :], c], axis=0)
    cc = jnp.pad(cc, ((0, MOD_ROWS - cc.shape[0]), (0, 0)))
    mods = _modulation(cc, w_mod, b_mod).reshape(DEPTH, MOD_ROWS, 6, D_MODEL)

    bcs = _channel_dft_table()
    cs_p = _position_dft_table(seq_p)
    cs_s = _position_dft_table(seq_s)
    rope = _rope_tables(seq_s)
    ck = cache_k.reshape(dec_batch, DEPTH, past, KV_W)
    cv = cache_v.reshape(dec_batch, DEPTH, past, KV_W)

    w_in_b, wa, wc, wf, wo, w_up_b, wd = (
        w.astype(BF16) for w in (w_in, w_attn_o, w_conv_o, w_fnet, w_o, w_up, w_down))
    b_in3 = b_in[:, None, :]
    cw = jnp.broadcast_to(conv_w[:, :, None, :], (DEPTH, CONV_K, SUBLANES, CONV_CH))
    cvec = jnp.stack([conv_b, conv_ln_g, conv_ln_b], axis=1)
    dvec = jnp.stack([b_fnet, ln1_g, ln1_b], axis=1)
    fw = jnp.broadcast_to(ffn_conv_w[:, :, None, :], (DEPTH, 3, SUBLANES, 2 * D_FF))
    fb = jnp.broadcast_to(ffn_conv_b[:, None, :], (DEPTH, SUBLANES, 2 * D_FF))
    lvec = jnp.stack([ln2_g, ln2_b], axis=1)

    xp = x_prompt.reshape(batch * seq_p, D_MODEL)
    xs = x_sample.reshape(dec_batch * seq_s, D_MODEL)
    new_k, new_v = [], []
    for l in range(DEPTH):
        def run(x, mod_off, mod_stride, seq, rope_tabs, attend, cs):
            q, k, v, g, fcs, gates = _inproj(x, mods, l, mod_off, mod_stride, seq, w_in_b, b_in3, bcs,
                                             rope_tabs)
            attn = attend(q, k, v)
            x = _mix(x, mods, l, mod_off, mod_stride, seq, attn, g, fcs, cs, gates, wa, wc, wf, wo,
                     cw, cvec, dvec)
            x = _ffn(x, mods, l, mod_off, mod_stride, seq, w_up_b, wd, fw, fb, lvec)
            return x, k, v

        xp, kp, vp = run(xp, 0, 0, seq_p, None,
                         lambda q, k, v: _ctx_attn(q, k, v, sink[l], seq_p), cs_p)
        new_k.append(kp.reshape(batch, seq_p, N_KV_HEADS, HEAD_DIM))
        new_v.append(vp.reshape(batch, seq_p, N_KV_HEADS, HEAD_DIM))
        xs, _, _ = run(xs, 1, 1, seq_s, rope,
                       lambda q, k, v: _win_attn(q, k, v, ck, cv, l, sink[l], seq_s), cs_s)

    return (xp.reshape(batch, seq_p, D_MODEL), xs.reshape(dec_batch, seq_s, D_MODEL),
            jnp.stack(new_k, axis=1), jnp.stack(new_v, axis=1))
```
